```python
import jax, jax.numpy as jnp
from jax import lax
import numpy as np

D_MODEL = 1024
BATCH = 8
SEQ = 2048
DEPTH = 4
DEC_BATCH = 128
DEC_SEQ = 4
PAST_LEN = 16384
PAGE_SIZE = 128

N_META = 16
D_MIX = D_MODEL
D_POOL = D_MIX // 2
D_REC = D_MIX - D_POOL
POOL_WINDOWS = (2, 4, 8, 16)
N_POOL_GROUPS = len(POOL_WINDOWS)
POOL_GROUP = D_POOL // N_POOL_GROUPS
POOL_BUF = max(POOL_WINDOWS) - 1
REC_HEAD_F = 128
N_REC_HEADS = D_REC // REC_HEAD_F
REC_HEAD_I = D_REC // N_REC_HEADS
REC_CHUNK = 64
D_FF = 2816
CONV_W = 3
D_IN = D_POOL + 4 * D_REC
EPS = 1e-6

kernel_name = 'hymba_pool_hgrn2_convffn_step'


def rmsnorm(x, g):
    xf = x.astype(jnp.float32)
    y = xf * lax.rsqrt(jnp.mean(xf * xf, axis=-1, keepdims=True) + EPS)
    return (y * g.astype(jnp.float32)).astype(x.dtype)


def pool_mix(u, prev, start, w_pool, scale):
    N, T, _ = u.shape
    uf = u.astype(jnp.float32)
    ext = jnp.concatenate([prev.astype(jnp.float32), uf], axis=1)
    c = jnp.concatenate([jnp.zeros((N, 1, D_POOL), jnp.float32), jnp.cumsum(ext, axis=1)], axis=1)
    pos = start + jnp.arange(T)
    P = POOL_BUF
    outs = []
    for gi, w in enumerate(POOL_WINDOWS):
        cs = c[:, :, gi * POOL_GROUP:(gi + 1) * POOL_GROUP]
        win_sum = cs[:, P + 1:P + 1 + T] - cs[:, P + 1 - w:P + 1 - w + T]
        cnt = jnp.minimum(pos + 1, w).astype(jnp.float32)
        outs.append(win_sum / cnt[None, :, None] - uf[:, :, gi * POOL_GROUP:(gi + 1) * POOL_GROUP])
    m = jnp.stack(outs, axis=2)
    y = jnp.einsum('ntgc,gcd->ntgd', m, w_pool.astype(jnp.float32)).reshape(N, T, D_POOL)
    y = y * scale.astype(jnp.float32)
    return y.astype(u.dtype), ext[:, -POOL_BUF:].astype(prev.dtype)


def gla_chunk(S, q, k, g, v):
    C = q.shape[2]
    G = jnp.cumsum(g, axis=2)
    causal = jnp.tril(jnp.ones((C, C), bool))
    diff = G[:, :, :, None, :] - G[:, :, None, :, :]
    decay = jnp.exp(jnp.where(causal[:, :, None], diff, -jnp.inf))
    A = jnp.einsum('nhtd,nhsd,nhtsd->nhts', q, k, decay)
    o = jnp.einsum('nhts,nhsv->nhtv', A, v) + jnp.einsum('nhtd,nhdv->nhtv', q * jnp.exp(G), S)
    G_last = G[:, :, -1:, :]
    S_new = jnp.exp(G_last[:, :, 0, :])[..., None] * S + jnp.einsum('nhsd,nhsv->nhdv', k * jnp.exp(G_last - G), v)
    return o, S_new


def hgrn2_mix(z, S0, lb, g_norm, lead):
    N, T, _ = z.shape
    zf = z.astype(jnp.float32)
    q, f, i, gate = jnp.split(zf, 4, axis=-1)
    q = jax.nn.silu(q) * REC_HEAD_F ** -0.5
    log_f = jnp.logaddexp(jnp.log(lb), jnp.log1p(-lb) + jax.nn.log_sigmoid(f))
    k = (1.0 - lb) * jax.nn.sigmoid(-f)
    heads = lambda a, d: a.reshape(N, T, N_REC_HEADS, d).transpose(0, 2, 1, 3)
    q, k, log_f = heads(q, REC_HEAD_F), heads(k, REC_HEAD_F), heads(log_f, REC_HEAD_F)
    v = heads(i, REC_HEAD_I)
    S = S0.astype(jnp.float32)
    o, S = gla_chunk(S, q[:, :, :lead], k[:, :, :lead], log_f[:, :, :lead], v[:, :, :lead])
    rest = T - lead
    if rest > 0:
        n_c = rest // REC_CHUNK
        blocks = lambda a: a[:, :, lead:].reshape(N, N_REC_HEADS, n_c, REC_CHUNK, a.shape[-1]).transpose(2, 0, 1, 3, 4)

        def step(S, xs):
            o_c, S = gla_chunk(S, *xs)
            return S, o_c

        S, o_rest = lax.scan(step, S, (blocks(q), blocks(k), blocks(log_f), blocks(v)))
        o_rest = o_rest.transpose(1, 2, 0, 3, 4).reshape(N, N_REC_HEADS, rest, REC_HEAD_I)
        o = jnp.concatenate([o, o_rest], axis=2)
    o = o.transpose(0, 2, 1, 3)
    o = o * lax.rsqrt(jnp.mean(o * o, axis=-1, keepdims=True) + EPS) * g_norm.astype(jnp.float32)
    o = o.reshape(N, T, D_REC) * jax.nn.silu(gate)
    return o.astype(z.dtype), S


def conv_ffn(h, prev, w_a, w_b, conv_w, conv_b, w_down):
    T = h.shape[1]
    a = h @ w_a
    b = h @ w_b
    ext = jnp.concatenate([prev.astype(a.dtype), a], axis=1)
    ac = conv_b + sum(conv_w[j] * ext[:, j:j + T] for j in range(CONV_W))
    y = (jax.nn.gelu(ac, approximate=False) * b) @ w_down
    return y, ext[:, -(CONV_W - 1):].astype(prev.dtype)


def trunk_layer(x, pool_prev, pool_start, S0, conv_prev, lead, n1, w_in, w_pool, pool_scale, lb, g_norm,
                w_out, n2, w_a, w_b, conv_w, conv_b, w_down):
    h = rmsnorm(x, n1)
    z = h @ w_in
    y_pool, pool_new = pool_mix(z[..., :D_POOL], pool_prev, pool_start, w_pool, pool_scale)
    y_rec, S_new = hgrn2_mix(z[..., D_POOL:], S0, lb, g_norm, lead)
    x = x + jnp.concatenate([y_pool, y_rec], axis=-1) @ w_out
    y_ffn, conv_new = conv_ffn(rmsnorm(x, n2), conv_prev, w_a, w_b, conv_w, conv_b, w_down)
    return x + y_ffn, pool_new, S_new, conv_new


def setup_inputs(seed: int = 0) -> dict:
    key = jax.random.key(seed)
    ks = jax.random.split(key, 21)
    nrm = lambda k, shape, s: jax.random.normal(k, shape, jnp.float32) * s
    return {
        'x_prompt': nrm(ks[0], (BATCH, SEQ, D_MODEL), 1.0),
        'x_sample': nrm(ks[1], (DEC_BATCH, DEC_SEQ, D_MODEL), 1.0),
        'state_pool': nrm(ks[2], (DEPTH, DEC_BATCH, POOL_BUF, D_POOL), 1.0),
        'state_rec': nrm(ks[3], (DEPTH, DEC_BATCH, N_REC_HEADS, REC_HEAD_F, REC_HEAD_I), 0.3),
        'state_conv': nrm(ks[4], (DEPTH, DEC_BATCH, CONV_W - 1, D_FF), 1.0),
        'meta_tokens': nrm(ks[5], (N_META, D_MODEL), 1.0),
        'norm1_g': 1.0 + nrm(ks[6], (DEPTH, D_MODEL), 0.02),
        'w_in': nrm(ks[7], (DEPTH, D_MODEL, D_IN), D_MODEL ** -0.5),
        'w_pool': nrm(ks[8], (DEPTH, N_POOL_GROUPS, POOL_GROUP, POOL_GROUP), POOL_GROUP ** -0.5),
        'pool_scale': 1.0 + nrm(ks[9], (DEPTH, D_POOL), 0.02),
        'lb_raw': nrm(ks[10], (DEPTH, D_REC), 0.1),
        'rec_norm_g': 1.0 + nrm(ks[11], (DEPTH, REC_HEAD_I), 0.02),
        'w_out': nrm(ks[12], (DEPTH, D_MIX, D_MODEL), D_MIX ** -0.5),
        'norm2_g': 1.0 + nrm(ks[13], (DEPTH, D_MODEL), 0.02),
        'w_a': nrm(ks[14], (DEPTH, D_MODEL, D_FF), D_MODEL ** -0.5),
        'w_b': nrm(ks[15], (DEPTH, D_MODEL, D_FF), D_MODEL ** -0.5),
        'conv_w': nrm(ks[16], (DEPTH, CONV_W, D_FF), CONV_W ** -0.5),
        'conv_b': nrm(ks[17], (DEPTH, D_FF), 0.01),
        'w_down': nrm(ks[18], (DEPTH, D_FF, D_MODEL), D_FF ** -0.5),
        'final_g': 1.0 + nrm(ks[19], (D_MODEL,), 0.02),
    }


def reference(x_prompt, x_sample, state_pool, state_rec, state_conv, meta_tokens, norm1_g, w_in, w_pool,
              pool_scale, lb_raw, rec_norm_g, w_out, norm2_g, w_a, w_b, conv_w, conv_b, w_down, final_g):
    lb_all = jnp.cumsum(jax.nn.softmax(lb_raw.astype(jnp.float32), axis=0), axis=0)
    lb_all = lb_all - lb_all[:1]
    B = x_prompt.shape[0]
    xp = jnp.concatenate([jnp.broadcast_to(meta_tokens[None].astype(x_prompt.dtype), (B, N_META, D_MODEL)), x_prompt], axis=1)
    xs = x_sample
    pool_p0 = jnp.zeros((B, POOL_BUF, D_POOL), state_pool.dtype)
    rec_p0 = jnp.zeros((B, N_REC_HEADS, REC_HEAD_F, REC_HEAD_I), jnp.float32)
    conv_p0 = jnp.zeros((B, CONV_W - 1, D_FF), state_conv.dtype)
    pool_p, rec_p, conv_p, pool_s, rec_s, conv_s = [], [], [], [], [], []
    for l in range(DEPTH):
        lp = (norm1_g[l], w_in[l], w_pool[l], pool_scale[l], lb_all[l], rec_norm_g[l], w_out[l], norm2_g[l],
              w_a[l], w_b[l], conv_w[l], conv_b[l], w_down[l])
        xp, pp, sp, cp = trunk_layer(xp, pool_p0, 0, rec_p0, conv_p0, N_META, *lp)
        xs, ps, ss, cs = trunk_layer(xs, state_pool[l], PAST_LEN, state_rec[l], state_conv[l], xs.shape[1], *lp)
        pool_p.append(pp); rec_p.append(sp.astype(state_rec.dtype)); conv_p.append(cp)
        pool_s.append(ps); rec_s.append(ss.astype(state_rec.dtype)); conv_s.append(cs)
    y_prompt = rmsnorm(xp, final_g)[:, N_META:]
    y_sample = rmsnorm(xs, final_g)
    return (y_prompt, y_sample, jnp.stack(pool_p), jnp.stack(rec_p), jnp.stack(conv_p),
            jnp.stack(pool_s), jnp.stack(rec_s), jnp.stack(conv_s))
```

```python
import functools

import jax
import jax.numpy as jnp
from jax import lax
from jax.experimental import pallas as pl
from jax.experimental.pallas import tpu as pltpu

F32 = jnp.float32
BF16 = jnp.bfloat16

D_MODEL = 1024
DEPTH = 4
D_POOL = 512
D_REC = 512
N_HEADS = 4
HEAD = 128
D_IN = D_POOL + 4 * D_REC
D_FF = 2816
POOL_WINDOWS = (2, 4, 8, 16)
POOL_GROUP = 128
POOL_BUF = 15
N_META = 16
DEC_BATCH = 128
DEC_SEQ = 4
CONV_W = 3
EPS = 1e-6
Q_SCALE = HEAD ** -0.5
SQRT_HALF = 0.7071067811865476

SUBLANES = 8
HIST = 16
CONV_HIST = 8
VMEM_LIMIT = 56 * 1024 * 1024


def _rmsnorm(x, g):
    return x * lax.rsqrt(jnp.mean(x * x, axis=-1, keepdims=True) + EPS) * g


def _silu(x):
    return x * (1.0 / (1.0 + jnp.exp(-x)))


def _gelu(x):
    return 0.5 * x * (1.0 + lax.erf(x * SQRT_HALF))


def _lower_bound(lb_raw, layer):
    m = jnp.max(lb_raw, axis=0, keepdims=True)
    e = jnp.exp(lb_raw - m)
    p = e / jnp.sum(e, axis=0, keepdims=True)
    lb = jnp.zeros_like(m)
    for i in range(1, layer + 1):
        lb = lb + p[i:i + 1]
    return lb


def _forget_gate(f, lb):
    m = jnp.exp(-jnp.abs(f))
    r = 1.0 / (1.0 + m)
    sig_neg = jnp.where(f >= 0, m * r, r)
    log_sig = jnp.minimum(f, 0.0) - jnp.log(1.0 + m)
    a = jnp.log(lb)
    b = jnp.log(1.0 - lb) + log_sig
    log_f = jnp.maximum(a, b) + jnp.log(1.0 + jnp.exp(-jnp.abs(a - b)))
    return log_f, (1.0 - lb) * sig_neg


def _dot(a, b):
    return jnp.dot(a, b, preferred_element_type=F32)


def _dot_nt(a, b):
    return lax.dot_general(a, b, (((1,), (1,)), ((), ())), preferred_element_type=F32)


def _dot_tn(a, b):
    return lax.dot_general(a, b, (((0,), (0,)), ((), ())), preferred_element_type=F32)


def _gla_chunk(q, k, g, v, st, n_blk):
    c_len = SUBLANES * n_blk
    sub = lax.broadcasted_iota(jnp.int32, (SUBLANES, HEAD), 0)
    blk = lambda a, i: a[SUBLANES * i:SUBLANES * (i + 1)]

    local, prefix = [], [jnp.zeros((SUBLANES, HEAD), F32)]
    for i in range(n_blk):
        x = blk(g, i)
        for s in (1, 2, 4):
            x = x + jnp.where(sub >= s, pltpu.roll(x, s, 0), 0.0)
        local.append(x)
        prefix.append(prefix[-1] + jnp.broadcast_to(x[SUBLANES - 1:SUBLANES], (SUBLANES, HEAD)))
    g_cum = [prefix[i] + local[i] for i in range(n_blk)]
    g_last = prefix[n_blk]

    q_dec = jnp.concatenate([blk(q, i) * jnp.exp(g_cum[i]) for i in range(n_blk)], axis=0)
    k_dec = jnp.concatenate([blk(k, i) * jnp.exp(g_last - g_cum[i]) for i in range(n_blk)], axis=0)

    level_prod = {}
    half = n_blk // 2
    while half >= 1:
        ql, kl = [], []
        for i in range(n_blk):
            cut = (i // (2 * half)) * 2 * half + half
            e = jnp.exp(-jnp.abs(g_cum[i] - prefix[cut]))
            ql.append(blk(q, i) * e)
            kl.append(blk(k, i) * e)
        level_prod[half] = _dot_nt(jnp.concatenate(ql, axis=0).astype(BF16),
                                   jnp.concatenate(kl, axis=0).astype(BF16))
        half //= 2

    lane = lax.broadcasted_iota(jnp.int32, (SUBLANES, c_len), 1)
    diag = lane - lax.broadcasted_iota(jnp.int32, (SUBLANES, c_len), 0)
    rows = []
    for i in range(n_blk):
        qi, ki = blk(q, i), blk(k, i)
        gate = jnp.exp(blk(g, i))
        w = qi
        a = jnp.sum(w * ki, axis=-1, keepdims=True)
        row = jnp.where(diag == SUBLANES * i, a, 0.0)
        for r in range(1, SUBLANES):
            w = w * pltpu.roll(gate, r - 1, 0)
            a = jnp.sum(w * pltpu.roll(ki, r, 0), axis=-1, keepdims=True)
            row = jnp.where(diag == SUBLANES * i - r, a, row)
        half = 1
        while half < n_blk:
            if i % (2 * half) >= half:
                start = (i // (2 * half)) * 2 * half
                row = jnp.where(lane < SUBLANES * (start + half), blk(level_prod[half], i), row)
            half *= 2
        rows.append(row)
    att = jnp.concatenate(rows, axis=0)

    st_b = st.astype(BF16)
    v_b = v.astype(BF16)
    out = _dot(att.astype(BF16), v_b) + _dot_nt(q_dec.astype(BF16), st_b)
    st_new = st * jnp.exp(g_last[0:1]) + _dot_tn(v_b, k_dec.astype(BF16))
    return out, st_new


def _pool_count(start, tile_rows, tile_idx, w):
    if start >= max(POOL_WINDOWS) - 1:
        return 1.0 / w
    pos = start + tile_idx * tile_rows + lax.broadcasted_iota(jnp.int32, (tile_rows, 1), 0)
    return 1.0 / jnp.minimum(pos + 1, w).astype(F32)


def _seq_mixer_kernel(x_ref, n1_ref, win_ref, wpool_ref, pscale_ref, lbraw_ref, gnorm_ref, wout_ref,
                      pool0_ref, s0_ref, xo_ref, poolo_ref, so_ref, z_ref, st_ref, mix_ref,
                      *, layer, tile, chunk, start):
    t = pl.program_id(1)
    n_chunks = tile // chunk
    n_blk = chunk // SUBLANES

    @pl.when(t == 0)
    def _():
        z_ref[0:HIST, 0:D_POOL] = pool0_ref[...]
        for h in range(N_HEADS):
            st_ref[h] = s0_ref[h].T

    x = x_ref[0]
    hn = _rmsnorm(x, n1_ref[...]).astype(BF16)
    z_ref[HIST:HIST + tile, :] = _dot(hn, win_ref[...])

    for gi, w in enumerate(POOL_WINDOWS):
        lanes = slice(gi * POOL_GROUP, (gi + 1) * POOL_GROUP)
        acc = z_ref[HIST:HIST + tile, lanes]
        u = acc
        for j in range(1, w):
            acc = acc + z_ref[HIST - j:HIST - j + tile, lanes]
        m = acc * _pool_count(start, tile, t, w) - u
        y = _dot(m.astype(BF16), wpool_ref[gi]) * pscale_ref[:, lanes]
        mix_ref[:, lanes] = y.astype(BF16)

    @pl.when(t == pl.num_programs(1) - 1)
    def _():
        poolo_ref[0] = z_ref[HIST + tile - POOL_BUF:HIST + tile, 0:D_POOL]

    z_ref[0:HIST, 0:D_POOL] = z_ref[tile:tile + HIST, 0:D_POOL]

    lb = _lower_bound(lbraw_ref[...], layer)
    gnorm = gnorm_ref[...]

    def chunk_body(c, carry):
        r0 = pl.multiple_of(HIST + c * chunk, SUBLANES)
        m0 = pl.multiple_of(c * chunk, chunk)
        for h in range(N_HEADS):
            col = lambda part: slice(D_POOL + part * D_REC + h * HEAD, D_POOL + part * D_REC + (h + 1) * HEAD)
            hl = slice(h * HEAD, (h + 1) * HEAD)
            q = _silu(z_ref[pl.ds(r0, chunk), col(0)]) * Q_SCALE
            log_f, k = _forget_gate(z_ref[pl.ds(r0, chunk), col(1)], lb[:, hl])
            v = z_ref[pl.ds(r0, chunk), col(2)]
            o, st_new = _gla_chunk(q, k, log_f, v, st_ref[h], n_blk)
            st_ref[h] = st_new
            o = o * lax.rsqrt(jnp.mean(o * o, axis=-1, keepdims=True) + EPS) * gnorm
            o = o * _silu(z_ref[pl.ds(r0, chunk), col(3)])
            mix_ref[pl.ds(m0, chunk), D_POOL + h * HEAD:D_POOL + (h + 1) * HEAD] = o.astype(BF16)
        return carry

    lax.fori_loop(0, n_chunks, chunk_body, 0)

    xo_ref[0] = x + _dot(mix_ref[...], wout_ref[...])

    @pl.when(t == pl.num_programs(1) - 1)
    def _():
        for h in range(N_HEADS):
            so_ref[0, h] = st_ref[h].T


def _layer_spec(shape, layer, single=True):
    zeros = (0,) * len(shape)
    mode = dict(pipeline_mode=pl.Buffered(1)) if single else {}
    return pl.BlockSpec((None,) + tuple(shape), lambda *_: (layer,) + zeros, **mode)


def _const_spec(shape):
    zeros = (0,) * len(shape)
    return pl.BlockSpec(tuple(shape), lambda *_: zeros, pipeline_mode=pl.Buffered(1))


def _seq_mixer(x, p, layer, pool0, s0, *, tile, chunk, start, name):
    n_seq, t_len, _ = x.shape
    grid = (n_seq, t_len // tile)
    kern = functools.partial(_seq_mixer_kernel, layer=layer, tile=tile, chunk=chunk, start=start)
    return pl.pallas_call(
        kern,
        grid=grid,
        in_specs=[
            pl.BlockSpec((1, tile, D_MODEL), lambda b, t: (b, t, 0)),
            _layer_spec((1, D_MODEL), layer),
            _layer_spec((D_MODEL, D_IN), layer),
            _layer_spec((len(POOL_WINDOWS), POOL_GROUP, POOL_GROUP), layer),
            _layer_spec((1, D_POOL), layer),
            _const_spec((DEPTH, D_REC)),
            _layer_spec((1, HEAD), layer),
            _layer_spec((D_MODEL, D_MODEL), layer),
            _const_spec((HIST, D_POOL)),
            _const_spec((N_HEADS, HEAD, HEAD)),
        ],
        out_specs=[
            pl.BlockSpec((1, tile, D_MODEL), lambda b, t: (b, t, 0)),
            pl.BlockSpec((1, POOL_BUF, D_POOL), lambda b, t: (b, 0, 0)),
            pl.BlockSpec((1, N_HEADS, HEAD, HEAD), lambda b, t: (b, 0, 0, 0)),
        ],
        out_shape=[
            jax.ShapeDtypeStruct((n_seq, t_len, D_MODEL), F32),
            jax.ShapeDtypeStruct((n_seq, POOL_BUF, D_POOL), F32),
            jax.ShapeDtypeStruct((n_seq, N_HEADS, HEAD, HEAD), F32),
        ],
        scratch_shapes=[
            pltpu.VMEM((HIST + tile, D_IN), F32),
            pltpu.VMEM((N_HEADS, HEAD, HEAD), F32),
            pltpu.VMEM((tile, D_MODEL), BF16),
        ],
        compiler_params=pltpu.CompilerParams(
            dimension_semantics=("arbitrary", "arbitrary"), vmem_limit_bytes=VMEM_LIMIT),
        name=name,
    )(x, p["n1"], p["w_in"], p["w_pool"], p["pool_scale"], p["lb_raw"], p["g_norm"], p["w_out"], pool0, s0)


def _seq_ffn_kernel(x_ref, n2_ref, wa_ref, wb_ref, cw_ref, cb_ref, wd_ref, conv0_ref, fg_ref,
                    xo_ref, convo_ref, a_ref, *, tile, final):
    t = pl.program_id(1)
    lo = CONV_HIST - (CONV_W - 1)

    @pl.when(t == 0)
    def _():
        a_ref[lo:CONV_HIST, :] = conv0_ref[...]

    x = x_ref[0]
    hn = _rmsnorm(x, n2_ref[...]).astype(BF16)
    a_ref[CONV_HIST:CONV_HIST + tile, :] = _dot(hn, wa_ref[...])
    b = _dot(hn, wb_ref[...])
    ac = cb_ref[...]
    for j in range(CONV_W):
        ac = ac + cw_ref[j:j + 1, :] * a_ref[lo + j:lo + j + tile, :]
    act = (_gelu(ac) * b).astype(BF16)
    xo = x + _dot(act, wd_ref[...])
    if final:
        xo = _rmsnorm(xo, fg_ref[...])
    xo_ref[0] = xo

    @pl.when(t == pl.num_programs(1) - 1)
    def _():
        convo_ref[0] = a_ref[lo + tile:CONV_HIST + tile, :]

    a_ref[lo:CONV_HIST, :] = a_ref[lo + tile:CONV_HIST + tile, :]


def _seq_ffn(x, p, layer, conv0, *, tile, final, name):
    n_seq, t_len, _ = x.shape
    kern = functools.partial(_seq_ffn_kernel, tile=tile, final=final)
    return pl.pallas_call(
        kern,
        grid=(n_seq, t_len // tile),
        in_specs=[
            pl.BlockSpec((1, tile, D_MODEL), lambda b, t: (b, t, 0)),
            _layer_spec((1, D_MODEL), layer),
            _layer_spec((D_MODEL, D_FF), layer),
            _layer_spec((D_MODEL, D_FF), layer),
            _layer_spec((CONV_W, D_FF), layer),
            _layer_spec((1, D_FF), layer),
            _layer_spec((D_FF, D_MODEL), layer),
            _const_spec((CONV_W - 1, D_FF)),
            _const_spec((1, D_MODEL)),
        ],
        out_specs=[
            pl.BlockSpec((1, tile, D_MODEL), lambda b, t: (b, t, 0)),
            pl.BlockSpec((1, CONV_W - 1, D_FF), lambda b, t: (b, 0, 0)),
        ],
        out_shape=[
            jax.ShapeDtypeStruct((n_seq, t_len, D_MODEL), F32),
            jax.ShapeDtypeStruct((n_seq, CONV_W - 1, D_FF), F32),
        ],
        scratch_shapes=[pltpu.VMEM((CONV_HIST + tile, D_FF), F32)],
        compiler_params=pltpu.CompilerParams(
            dimension_semantics=("arbitrary", "arbitrary"), vmem_limit_bytes=VMEM_LIMIT),
        name=name,
    )(x, p["n2"], p["w_a"], p["w_b"], p["conv_w"], p["conv_b"], p["w_down"], conv0, p["final_g"])


S_ROWS = DEC_BATCH * DEC_SEQ
S_SLOT = SUBLANES * DEC_BATCH
N_SRC = DEC_SEQ + 3
SEQ_BLOCK = 8


def _sample_mixer_kernel(x_ref, n1_ref, win_ref, wpool_ref, pscale_ref, lbraw_ref, gnorm_ref, wout_ref,
                         sp_ref, s_ref, xo_ref, spo_ref, so_ref,
                         z_ref, mix_ref, oin_ref, src_ref, gather_ref, scatter_ref,
                         qd8_ref, kd8_ref, v8_ref, o8_ref, *, layer):
    step = pl.program_id(0)
    tok = lambda t: slice(t * DEC_BATCH, (t + 1) * DEC_BATCH)

    @pl.when(step == 0)
    def _():
        hn = _rmsnorm(x_ref[...], n1_ref[...]).astype(BF16)
        z_ref[...] = _dot(hn, win_ref[...])

        def ext(j, lanes):
            if j >= POOL_BUF:
                return z_ref[tok(j - POOL_BUF), lanes]
            return sp_ref[:, j * D_POOL + lanes.start:j * D_POOL + lanes.stop]

        for gi, w in enumerate(POOL_WINDOWS):
            lanes = slice(gi * POOL_GROUP, (gi + 1) * POOL_GROUP)
            ms = []
            for t in range(DEC_SEQ):
                acc = ext(POOL_BUF + t, lanes)
                u = acc
                for j in range(1, w):
                    acc = acc + ext(POOL_BUF + t - j, lanes)
                ms.append(acc * (1.0 / w) - u)
            y = _dot(jnp.concatenate(ms, axis=0).astype(BF16), wpool_ref[gi]) * pscale_ref[:, lanes]
            mix_ref[:, lanes] = y
        full = slice(0, D_POOL)
        for j in range(POOL_BUF):
            spo_ref[:, j * D_POOL:(j + 1) * D_POOL] = ext(j + DEC_SEQ, full)

        lb = _lower_bound(lbraw_ref[...], layer)
        rec = lambda part: slice(D_POOL + part * D_REC, D_POOL + (part + 1) * D_REC)
        qs, ks, gs, vs = [], [], [], []
        for t in range(DEC_SEQ):
            qs.append(_silu(z_ref[tok(t), rec(0)]) * Q_SCALE)
            log_f, k = _forget_gate(z_ref[tok(t), rec(1)], lb)
            ks.append(k)
            gs.append(log_f if t == 0 else gs[-1] + log_f)
            vs.append(z_ref[tok(t), rec(2)])
        g_last = gs[-1]
        for t in range(DEC_SEQ):
            acc = None
            for s in range(t + 1):
                prod = qs[t] * ks[s]
                if s < t:
                    prod = prod * jnp.exp(gs[t] - gs[s])
                parts = []
                for h in range(N_HEADS):
                    hl = slice(h * HEAD, (h + 1) * HEAD)
                    a = jnp.sum(prod[:, hl], axis=-1, keepdims=True)
                    parts.append(a * vs[s][:, hl])
                term = jnp.concatenate(parts, axis=1)
                acc = term if acc is None else acc + term
            oin_ref[tok(t), :] = acc

        row = lax.broadcasted_iota(jnp.int32, (S_SLOT, N_SRC * DEC_BATCH), 0)
        col = lax.broadcasted_iota(jnp.int32, (S_SLOT, N_SRC * DEC_BATCH), 1)
        hit = ((row & (SUBLANES - 1)) == (col >> 7)) & ((row >> 3) == (col & (DEC_BATCH - 1)))
        gather_ref[...] = jnp.where(hit, 1.0, 0.0).astype(BF16)
        row = lax.broadcasted_iota(jnp.int32, (S_ROWS, S_SLOT), 0)
        col = lax.broadcasted_iota(jnp.int32, (S_ROWS, S_SLOT), 1)
        hit = ((col & (SUBLANES - 1)) == (row >> 7)) & ((col >> 3) == (row & (DEC_BATCH - 1)))
        scatter_ref[...] = jnp.where(hit, 1.0, 0.0).astype(BF16)

        for t in range(DEC_SEQ):
            src_ref[tok(t), :] = (ks[t] * jnp.exp(g_last - gs[t])).astype(BF16)
        rest = jnp.exp(g_last)
        for part in range(3):
            piece = rest.astype(BF16)
            src_ref[tok(DEC_SEQ + part), :] = piece
            rest = rest - piece.astype(F32)
        kd8_ref[...] = _dot(gather_ref[...], src_ref[...])
        for t in range(DEC_SEQ):
            src_ref[tok(t), :] = (qs[t] * jnp.exp(gs[t])).astype(BF16)
        qd8_ref[...] = _dot(gather_ref[:, 0:S_ROWS], src_ref[0:S_ROWS, :])
        for t in range(DEC_SEQ):
            src_ref[tok(t), :] = vs[t].astype(BF16)
        v8_ref[...] = _dot(gather_ref[:, 0:S_ROWS], src_ref[0:S_ROWS, :])

    slot = lax.broadcasted_iota(jnp.int32, (SUBLANES, HEAD), 0)
    ones8 = jnp.where((slot >= DEC_SEQ) & (slot < N_SRC), 1.0, 0.0).astype(BF16)
    base = pl.multiple_of(step * (SEQ_BLOCK * SUBLANES), SEQ_BLOCK * SUBLANES)
    for j in range(SEQ_BLOCK):
        rows = pl.ds(base + j * SUBLANES, SUBLANES)
        for h in range(N_HEADS):
            hl = slice(h * HEAD, (h + 1) * HEAD)
            st = s_ref[j, h]
            kd = kd8_ref[rows, hl].astype(BF16)
            o8_ref[rows, hl] = _dot(qd8_ref[rows, hl].astype(BF16), st.astype(BF16))
            so_ref[j, h] = _dot_tn(kd, ones8) * st + _dot_tn(kd, v8_ref[rows, hl].astype(BF16))

    @pl.when(step == pl.num_programs(0) - 1)
    def _():
        o8 = o8_ref[...]
        hi = o8.astype(BF16)
        lo = (o8 - hi.astype(F32)).astype(BF16)
        o = oin_ref[...] + _dot(scatter_ref[...], hi) + _dot(scatter_ref[...], lo)
        gate = _silu(z_ref[:, D_POOL + 3 * D_REC:D_IN])
        for h in range(N_HEADS):
            hl = slice(h * HEAD, (h + 1) * HEAD)
            oh = o[:, hl]
            oh = oh * lax.rsqrt(jnp.mean(oh * oh, axis=-1, keepdims=True) + EPS) * gnorm_ref[...]
            mix_ref[:, D_POOL + h * HEAD:D_POOL + (h + 1) * HEAD] = oh * gate[:, hl]
        xo_ref[...] = x_ref[...] + _dot(mix_ref[...].astype(BF16), wout_ref[...])


def _sample_mixer(x, p, layer, state_pool, state_rec, *, name):
    kern = functools.partial(_sample_mixer_kernel, layer=layer)
    rec_block = (None, SEQ_BLOCK, N_HEADS, HEAD, HEAD)
    return pl.pallas_call(
        kern,
        grid=(DEC_BATCH // SEQ_BLOCK,),
        in_specs=[
            _const_spec((S_ROWS, D_MODEL)),
            _layer_spec((1, D_MODEL), layer),
            _layer_spec((D_MODEL, D_IN), layer),
            _layer_spec((len(POOL_WINDOWS), POOL_GROUP, POOL_GROUP), layer),
            _layer_spec((1, D_POOL), layer),
            _const_spec((DEPTH, D_REC)),
            _layer_spec((1, HEAD), layer),
            _layer_spec((D_MODEL, D_MODEL), layer),
            _layer_spec((DEC_BATCH, POOL_BUF * D_POOL), layer),
            pl.BlockSpec(rec_block, lambda s: (layer, s, 0, 0, 0)),
        ],
        out_specs=[
            pl.BlockSpec((S_ROWS, D_MODEL), lambda s: (0, 0)),
            pl.BlockSpec((DEC_BATCH, POOL_BUF * D_POOL), lambda s: (0, 0)),
            pl.BlockSpec(rec_block[1:], lambda s: (s, 0, 0, 0)),
        ],
        out_shape=[
            jax.ShapeDtypeStruct((S_ROWS, D_MODEL), F32),
            jax.ShapeDtypeStruct((DEC_BATCH, POOL_BUF * D_POOL), F32),
            jax.ShapeDtypeStruct((DEC_BATCH, N_HEADS, HEAD, HEAD), F32),
        ],
        scratch_shapes=[
            pltpu.VMEM((S_ROWS, D_IN), F32),
            pltpu.VMEM((S_ROWS, D_MODEL), F32),
            pltpu.VMEM((S_ROWS, D_REC), F32),
            pltpu.VMEM((N_SRC * DEC_BATCH, D_REC), BF16),
            pltpu.VMEM((S_SLOT, N_SRC * DEC_BATCH), BF16),
            pltpu.VMEM((S_ROWS, S_SLOT), BF16),
            pltpu.VMEM((S_SLOT, D_REC), F32),
            pltpu.VMEM((S_SLOT, D_REC), F32),
            pltpu.VMEM((S_SLOT, D_REC), F32),
            pltpu.VMEM((S_SLOT, D_REC), F32),
        ],
        compiler_params=pltpu.CompilerParams(
            dimension_semantics=("arbitrary",), vmem_limit_bytes=VMEM_LIMIT),
        name=name,
    )(x, p["n1"], p["w_in"], p["w_pool"], p["pool_scale"], p["lb_raw"], p["g_norm"], p["w_out"],
      state_pool, state_rec)


def _sample_ffn_kernel(x_ref, n2_ref, wa_ref, wb_ref, cw_ref, cb_ref, wd_ref, cs_ref, fg_ref,
                       xo_ref, cso_ref, *, final):
    tok = lambda t: slice(t * DEC_BATCH, (t + 1) * DEC_BATCH)
    x = x_ref[...]
    hn = _rmsnorm(x, n2_ref[...]).astype(BF16)
    a = _dot(hn, wa_ref[...])
    b = _dot(hn, wb_ref[...])
    ext = [cs_ref[:, j * D_FF:(j + 1) * D_FF] for j in range(CONV_W - 1)] + [a[tok(t)] for t in range(DEC_SEQ)]
    acs = []
    for t in range(DEC_SEQ):
        ac = cb_ref[...]
        for j in range(CONV_W):
            ac = ac + cw_ref[j:j + 1, :] * ext[t + j]
        acs.append(ac)
    act = (_gelu(jnp.concatenate(acs, axis=0)) * b).astype(BF16)
    xo = x + _dot(act, wd_ref[...])
    if final:
        xo = _rmsnorm(xo, fg_ref[...])
    xo_ref[...] = xo
    for j in range(CONV_W - 1):
        cso_ref[:, j * D_FF:(j + 1) * D_FF] = ext[DEC_SEQ + j]


def _sample_ffn(x, p, layer, state_conv, *, final, name):
    kern = functools.partial(_sample_ffn_kernel, final=final)
    return pl.pallas_call(
        kern,
        grid=(1,),
        in_specs=[
            _const_spec((S_ROWS, D_MODEL)),
            _layer_spec((1, D_MODEL), layer),
            _layer_spec((D_MODEL, D_FF), layer),
            _layer_spec((D_MODEL, D_FF), layer),
            _layer_spec((CONV_W, D_FF), layer),
            _layer_spec((1, D_FF), layer),
            _layer_spec((D_FF, D_MODEL), layer),
            _layer_spec((DEC_BATCH, (CONV_W - 1) * D_FF), layer),
            _const_spec((1, D_MODEL)),
        ],
        out_specs=[
            pl.BlockSpec((S_ROWS, D_MODEL), lambda s: (0, 0)),
            pl.BlockSpec((DEC_BATCH, (CONV_W - 1) * D_FF), lambda s: (0, 0)),
        ],
        out_shape=[
            jax.ShapeDtypeStruct((S_ROWS, D_MODEL), F32),
            jax.ShapeDtypeStruct((DEC_BATCH, (CONV_W - 1) * D_FF), F32),
        ],
        compiler_params=pltpu.CompilerParams(
            dimension_semantics=("arbitrary",), vmem_limit_bytes=VMEM_LIMIT),
        name=name,
    )(x, p["n2"], p["w_a"], p["w_b"], p["conv_w"], p["conv_b"], p["w_down"], state_conv, p["final_g"])


PROMPT_TILE = 512
PROMPT_CHUNK = 64


def kernel(x_prompt, x_sample, state_pool, state_rec, state_conv, meta_tokens, norm1_g, w_in, w_pool, pool_scale, lb_raw, rec_norm_g, w_out, norm2_g, w_a, w_b, conv_w, conv_b, w_down, final_g):
    p = dict(
        n1=norm1_g.reshape(DEPTH, 1, D_MODEL), w_in=w_in.astype(BF16), w_pool=w_pool.astype(BF16),
        pool_scale=pool_scale.reshape(DEPTH, 1, D_POOL), lb_raw=lb_raw,
        g_norm=rec_norm_g.reshape(DEPTH, 1, HEAD), w_out=w_out.astype(BF16),
        n2=norm2_g.reshape(DEPTH, 1, D_MODEL), w_a=w_a.astype(BF16), w_b=w_b.astype(BF16),
        conv_w=conv_w, conv_b=conv_b.reshape(DEPTH, 1, D_FF), w_down=w_down.astype(BF16),
        final_g=final_g.reshape(1, D_MODEL),
    )
    last = DEPTH - 1

    xm = meta_tokens.reshape(1, N_META, D_MODEL)
    zero_pool = jnp.zeros((HIST, D_POOL), F32)
    zero_rec = jnp.zeros((N_HEADS, HEAD, HEAD), F32)
    zero_conv = jnp.zeros((CONV_W - 1, D_FF), F32)
    meta_states = []
    for l in range(DEPTH):
        xm, pool_m, rec_m = _seq_mixer(xm, p, l, zero_pool, zero_rec, tile=N_META, chunk=N_META, start=0,
                                       name=f"meta_mixer_{l}")
        xm, conv_m = _seq_ffn(xm, p, l, zero_conv, tile=N_META, final=False, name=f"meta_ffn_{l}")
        meta_states.append((jnp.pad(pool_m[0], ((HIST - POOL_BUF, 0), (0, 0))), rec_m[0], conv_m[0]))

    xp = x_prompt
    pool_p, rec_p, conv_p = [], [], []
    for l in range(DEPTH):
        pool0, rec0, conv0 = meta_states[l]
        xp, pp, sp = _seq_mixer(xp, p, l, pool0, rec0, tile=PROMPT_TILE, chunk=PROMPT_CHUNK, start=N_META,
                                name=f"prompt_mixer_{l}")
        xp, cp = _seq_ffn(xp, p, l, conv0, tile=PROMPT_TILE, final=(l == last), name=f"prompt_ffn_{l}")
        pool_p.append(pp)
        rec_p.append(sp)
        conv_p.append(cp)

    xs = x_sample.transpose(1, 0, 2).reshape(S_ROWS, D_MODEL)
    pool_flat = state_pool.reshape(DEPTH, DEC_BATCH, POOL_BUF * D_POOL)
    conv_flat = state_conv.reshape(DEPTH, DEC_BATCH, (CONV_W - 1) * D_FF)
    pool_s, rec_s, conv_s = [], [], []
    for l in range(DEPTH):
        xs, ps, ss = _sample_mixer(xs, p, l, pool_flat, state_rec, name=f"sample_mixer_{l}")
        xs, cs = _sample_ffn(xs, p, l, conv_flat, final=(l == last), name=f"sample_ffn_{l}")
        pool_s.append(ps.reshape(DEC_BATCH, POOL_BUF, D_POOL))
        rec_s.append(ss)
        conv_s.append(cs.reshape(DEC_BATCH, CONV_W - 1, D_FF))
    y_sample = xs.reshape(DEC_SEQ, DEC_BATCH, D_MODEL).transpose(1, 0, 2)

    return (xp, y_sample, jnp.stack(pool_p), jnp.stack(rec_p), jnp.stack(conv_p),
            jnp.stack(pool_s), jnp.stack(rec_s), jnp.stack(conv_s))
```

```python
import functools

import jax
import jax.numpy as jnp
from jax import lax
from jax.experimental import pallas as pl
from jax.experimental.pallas import tpu as pltpu

F32 = jnp.float32
BF16 = jnp.bfloat16

D_MODEL = 1024
DEPTH = 4
D_POOL = 512
D_REC = 512
N_HEADS = 4
HEAD = 128
N_PARTS = 4
D_IN = D_POOL + N_PARTS * D_REC
D_FF = 2816
POOL_WINDOWS = (2, 4, 8, 16)
POOL_GROUP = 128
POOL_BUF = 15
N_META = 16
DEC_BATCH = 128
DEC_SEQ = 4
CONV_W = 3
EPS = 1e-6
Q_SCALE = HEAD ** -0.5
SQRT_HALF = 0.7071067811865476

SUBLANES = 8
HIST = 16
CONV_HIST = 8
VMEM_LIMIT = 56 * 1024 * 1024


def _rmsnorm(x, g):
    return x * lax.rsqrt(jnp.mean(x * x, axis=-1, keepdims=True) + EPS) * g


def _silu(x):
    return x * (1.0 / (1.0 + jnp.exp(-x)))


def _gelu(x):
    return 0.5 * x * (1.0 + lax.erf(x * SQRT_HALF))


def _lower_bound(lb_raw, layer):
    m = jnp.max(lb_raw, axis=0, keepdims=True)
    e = jnp.exp(lb_raw - m)
    p = e / jnp.sum(e, axis=0, keepdims=True)
    lb = jnp.zeros_like(m)
    for i in range(1, layer + 1):
        lb = lb + p[i:i + 1]
    return lb


def _forget_gate(f, lb):
    m = jnp.exp(-jnp.abs(f))
    r = 1.0 / (1.0 + m)
    sig_neg = jnp.where(f >= 0, m * r, r)
    log_sig = jnp.minimum(f, 0.0) - jnp.log(1.0 + m)
    a = jnp.log(lb)
    b = jnp.log(1.0 - lb) + log_sig
    log_f = jnp.maximum(a, b) + jnp.log(1.0 + jnp.exp(-jnp.abs(a - b)))
    return log_f, (1.0 - lb) * sig_neg


def _dot(a, b):
    return jnp.dot(a, b, preferred_element_type=F32)


def _dot_nt(a, b):
    return lax.dot_general(a, b, (((1,), (1,)), ((), ())), preferred_element_type=F32)


def _dot_tn(a, b):
    return lax.dot_general(a, b, (((0,), (0,)), ((), ())), preferred_element_type=F32)


def _gla_chunk(q, k, g, v, st, n_blk):
    c_len = SUBLANES * n_blk
    sub = lax.broadcasted_iota(jnp.int32, (SUBLANES, HEAD), 0)
    blk = lambda a, i: a[SUBLANES * i:SUBLANES * (i + 1)]

    local, prefix = [], [jnp.zeros((SUBLANES, HEAD), F32)]
    for i in range(n_blk):
        x = blk(g, i)
        for s in (1, 2, 4):
            x = x + jnp.where(sub >= s, pltpu.roll(x, s, 0), 0.0)
        local.append(x)
        prefix.append(prefix[-1] + jnp.broadcast_to(x[SUBLANES - 1:SUBLANES], (SUBLANES, HEAD)))
    g_cum = [prefix[i] + local[i] for i in range(n_blk)]
    g_last = prefix[n_blk]

    q_dec = jnp.concatenate([blk(q, i) * jnp.exp(g_cum[i]) for i in range(n_blk)], axis=0)
    k_dec = jnp.concatenate([blk(k, i) * jnp.exp(g_last - g_cum[i]) for i in range(n_blk)], axis=0)

    level_prod = {}
    half = n_blk // 2
    while half >= 1:
        ql, kl = [], []
        for i in range(n_blk):
            cut = (i // (2 * half)) * 2 * half + half
            e = jnp.exp(-jnp.abs(g_cum[i] - prefix[cut]))
            ql.append(blk(q, i) * e)
            kl.append(blk(k, i) * e)
        level_prod[half] = _dot_nt(jnp.concatenate(ql, axis=0).astype(BF16),
                                   jnp.concatenate(kl, axis=0).astype(BF16))
        half //= 2

    lane = lax.broadcasted_iota(jnp.int32, (SUBLANES, c_len), 1)
    diag = lane - lax.broadcasted_iota(jnp.int32, (SUBLANES, c_len), 0)
    rows = []
    for i in range(n_blk):
        qi, ki = blk(q, i), blk(k, i)
        gate = jnp.exp(blk(g, i))
        w = qi
        a = jnp.sum(w * ki, axis=-1, keepdims=True)
        row = jnp.where(diag == SUBLANES * i, a, 0.0)
        for r in range(1, SUBLANES):
            w = w * pltpu.roll(gate, r - 1, 0)
            a = jnp.sum(w * pltpu.roll(ki, r, 0), axis=-1, keepdims=True)
            row = jnp.where(diag == SUBLANES * i - r, a, row)
        half = 1
        while half < n_blk:
            if i % (2 * half) >= half:
                start = (i // (2 * half)) * 2 * half
                row = jnp.where(lane < SUBLANES * (start + half), blk(level_prod[half], i), row)
            half *= 2
        rows.append(row)
    att = jnp.concatenate(rows, axis=0)

    st_b = st.astype(BF16)
    v_b = v.astype(BF16)
    out = _dot(att.astype(BF16), v_b) + _dot_nt(q_dec.astype(BF16), st_b)
    st_new = st * jnp.exp(g_last[0:1]) + _dot_tn(v_b, k_dec.astype(BF16))
    return out, st_new


def _pool_count(start, offset, rows, w):
    if start >= max(POOL_WINDOWS) - 1:
        return 1.0 / w
    pos = start + offset + lax.broadcasted_iota(jnp.int32, (rows, 1), 0)
    return 1.0 / jnp.minimum(pos + 1, w).astype(F32)


def _rec_col(h, part):
    lo = D_POOL + (h * N_PARTS + part) * HEAD
    return slice(lo, lo + HEAD)


def _layer_spec(shape, layer):
    zeros = (0,) * len(shape)
    return pl.BlockSpec((None,) + tuple(shape), lambda *_: (layer,) + zeros, pipeline_mode=pl.Buffered(1))


def _const_spec(shape):
    zeros = (0,) * len(shape)
    return pl.BlockSpec(tuple(shape), lambda *_: zeros, pipeline_mode=pl.Buffered(1))


def _stacked(acc):
    if acc is None:
        return [], [], 0
    return [pl.BlockSpec(memory_space=pl.ANY)] * len(acc), list(acc), len(acc)


def _seq_mixer_kernel(*refs, n_alias, layer, tile, sub, chunk, start):
    (x_ref, n1_ref, win_ref, wpool_ref, pscale_ref, lbraw_ref, gnorm_ref, wout_ref, pool0_ref, s0_ref) = refs[:10]
    xo_ref, poolo_ref, so_ref, z_ref, u_ref, st_ref, mix_ref = refs[10 + n_alias:]
    t = pl.program_id(1)
    n_blk = chunk // SUBLANES

    @pl.when(t == 0)
    def _():
        u_ref[0:HIST, :] = pool0_ref[...]
        for h in range(N_HEADS):
            st_ref[h] = s0_ref[h].T

    lb = _lower_bound(lbraw_ref[...], layer)
    gnorm = gnorm_ref[...]

    def sub_body(i, carry):
        r0 = pl.multiple_of(i * sub, sub)
        x = x_ref[0, pl.ds(r0, sub), :]
        hn = _rmsnorm(x, n1_ref[...]).astype(BF16)
        z_ref[...] = _dot(hn, win_ref[...])

        u_ref[HIST:HIST + sub, :] = z_ref[:, 0:D_POOL]
        for gi, w in enumerate(POOL_WINDOWS):
            lanes = slice(gi * POOL_GROUP, (gi + 1) * POOL_GROUP)
            acc = u_ref[HIST:HIST + sub, lanes]
            u = acc
            for j in range(1, w):
                acc = acc + u_ref[HIST - j:HIST - j + sub, lanes]
            m = acc * _pool_count(start, t * tile + i * sub, sub, w) - u
            y = _dot(m.astype(BF16), wpool_ref[gi]) * pscale_ref[:, lanes]
            mix_ref[:, lanes] = y.astype(BF16)
        u_ref[0:HIST, :] = u_ref[sub:sub + HIST, :]

        for h in range(N_HEADS):
            hl = slice(h * HEAD, (h + 1) * HEAD)
            st = st_ref[h]
            for c in range(sub // chunk):
                rows = slice(c * chunk, (c + 1) * chunk)
                q = _silu(z_ref[rows, _rec_col(h, 0)]) * Q_SCALE
                log_f, k = _forget_gate(z_ref[rows, _rec_col(h, 1)], lb[:, hl])
                o, st = _gla_chunk(q, k, log_f, z_ref[rows, _rec_col(h, 2)], st, n_blk)
                o = o * lax.rsqrt(jnp.mean(o * o, axis=-1, keepdims=True) + EPS) * gnorm
                o = o * _silu(z_ref[rows, _rec_col(h, 3)])
                mix_ref[rows, D_POOL + h * HEAD:D_POOL + (h + 1) * HEAD] = o.astype(BF16)
            st_ref[h] = st

        xo_ref[0, pl.ds(r0, sub), :] = x + _dot(mix_ref[...], wout_ref[...])
        return carry

    lax.fori_loop(0, tile // sub, sub_body, 0)

    @pl.when(t == pl.num_programs(1) - 1)
    def _():
        poolo_ref[0] = u_ref[HIST - POOL_BUF:HIST, :]
        for h in range(N_HEADS):
            so_ref[0, h] = st_ref[h].T


def _seq_mixer(x, p, layer, pool0, s0, acc, *, tile, sub, chunk, start, name):
    n_seq, t_len, _ = x.shape
    alias_specs, alias_args, n_alias = _stacked(acc)
    n_in = 10
    kern = functools.partial(_seq_mixer_kernel, n_alias=n_alias, layer=layer, tile=tile, sub=sub,
                             chunk=chunk, start=start)
    return pl.pallas_call(
        kern,
        grid=(n_seq, t_len // tile),
        in_specs=[
            pl.BlockSpec((1, tile, D_MODEL), lambda b, t: (b, t, 0)),
            _layer_spec((1, D_MODEL), layer),
            _layer_spec((D_MODEL, D_IN), layer),
            _layer_spec((len(POOL_WINDOWS), POOL_GROUP, POOL_GROUP), layer),
            _layer_spec((1, D_POOL), layer),
            _const_spec((DEPTH, D_REC)),
            _layer_spec((1, HEAD), layer),
            _layer_spec((D_MODEL, D_MODEL), layer),
            _const_spec((HIST, D_POOL)),
            _const_spec((N_HEADS, HEAD, HEAD)),
        ] + alias_specs,
        out_specs=[
            pl.BlockSpec((1, tile, D_MODEL), lambda b, t: (b, t, 0)),
            pl.BlockSpec((None, 1, POOL_BUF, D_POOL), lambda b, t: (layer, b, 0, 0)),
            pl.BlockSpec((None, 1, N_HEADS, HEAD, HEAD), lambda b, t: (layer, b, 0, 0, 0)),
        ],
        out_shape=[
            jax.ShapeDtypeStruct((n_seq, t_len, D_MODEL), F32),
            jax.ShapeDtypeStruct((DEPTH, n_seq, POOL_BUF, D_POOL), F32),
            jax.ShapeDtypeStruct((DEPTH, n_seq, N_HEADS, HEAD, HEAD), F32),
        ],
        scratch_shapes=[
            pltpu.VMEM((sub, D_IN), F32),
            pltpu.VMEM((HIST + sub, D_POOL), F32),
            pltpu.VMEM((N_HEADS, HEAD, HEAD), F32),
            pltpu.VMEM((sub, D_MODEL), BF16),
        ],
        input_output_aliases={n_in + k: 1 + k for k in range(n_alias)},
        compiler_params=pltpu.CompilerParams(
            dimension_semantics=("arbitrary", "arbitrary"), vmem_limit_bytes=VMEM_LIMIT),
        name=name,
    )(x, p["n1"], p["w_in"], p["w_pool"], p["pool_scale"], p["lb_raw"], p["g_norm"], p["w_out"], pool0, s0,
      *alias_args)


def _seq_ffn_kernel(*refs, n_alias, tile, final):
    x_ref, n2_ref, wa_ref, wb_ref, cw_ref, cb_ref, wd_ref, conv0_ref, fg_ref = refs[:9]
    xo_ref, convo_ref, a_ref = refs[9 + n_alias:]
    t = pl.program_id(1)
    lo = CONV_HIST - (CONV_W - 1)

    @pl.when(t == 0)
    def _():
        a_ref[lo:CONV_HIST, :] = conv0_ref[...]

    x = x_ref[0]
    hn = _rmsnorm(x, n2_ref[...]).astype(BF16)
    a_ref[CONV_HIST:CONV_HIST + tile, :] = _dot(hn, wa_ref[...])
    b = _dot(hn, wb_ref[...])
    ac = cb_ref[...]
    for j in range(CONV_W):
        ac = ac + cw_ref[j:j + 1, :] * a_ref[lo + j:lo + j + tile, :]
    act = (_gelu(ac) * b).astype(BF16)
    xo = x + _dot(act, wd_ref[...])
    if final:
        xo = _rmsnorm(xo, fg_ref[...])
    xo_ref[0] = xo

    @pl.when(t == pl.num_programs(1) - 1)
    def _():
        convo_ref[0] = a_ref[lo + tile:CONV_HIST + tile, :]

    a_ref[lo:CONV_HIST, :] = a_ref[lo + tile:CONV_HIST + tile, :]


def _seq_ffn(x, p, layer, conv0, acc, *, tile, final, name):
    n_seq, t_len, _ = x.shape
    alias_specs, alias_args, n_alias = _stacked(acc)
    n_in = 9
    kern = functools.partial(_seq_ffn_kernel, n_alias=n_alias, tile=tile, final=final)
    return pl.pallas_call(
        kern,
        grid=(n_seq, t_len // tile),
        in_specs=[
            pl.BlockSpec((1, tile, D_MODEL), lambda b, t: (b, t, 0)),
            _layer_spec((1, D_MODEL), layer),
            _layer_spec((D_MODEL, D_FF), layer),
            _layer_spec((D_MODEL, D_FF), layer),
            _layer_spec((CONV_W, D_FF), layer),
            _layer_spec((1, D_FF), layer),
            _layer_spec((D_FF, D_MODEL), layer),
            _const_spec((CONV_W - 1, D_FF)),
            _const_spec((1, D_MODEL)),
        ] + alias_specs,
        out_specs=[
            pl.BlockSpec((1, tile, D_MODEL), lambda b, t: (b, t, 0)),
            pl.BlockSpec((None, 1, CONV_W - 1, D_FF), lambda b, t: (layer, b, 0, 0)),
        ],
        out_shape=[
            jax.ShapeDtypeStruct((n_seq, t_len, D_MODEL), F32),
            jax.ShapeDtypeStruct((DEPTH, n_seq, CONV_W - 1, D_FF), F32),
        ],
        scratch_shapes=[pltpu.VMEM((CONV_HIST + tile, D_FF), F32)],
        input_output_aliases={n_in + k: 1 + k for k in range(n_alias)},
        compiler_params=pltpu.CompilerParams(
            dimension_semantics=("arbitrary", "arbitrary"), vmem_limit_bytes=VMEM_LIMIT),
        name=name,
    )(x, p["n2"], p["w_a"], p["w_b"], p["conv_w"], p["conv_b"], p["w_down"], conv0, p["final_g"], *alias_args)


S_ROWS = DEC_BATCH * DEC_SEQ
S_SLOT = SUBLANES * DEC_BATCH
N_SRC = DEC_SEQ + 3
SEQ_BLOCK = 8


def _sample_mixer_kernel(*refs, n_alias, layer):
    (x_ref, n1_ref, win_ref, wpool_ref, pscale_ref, lbraw_ref, gnorm_ref, wout_ref, sp_ref, s_ref) = refs[:10]
    (xo_ref, spo_ref, so_ref, z_ref, mix_ref, oin_ref, src_ref, gather_ref, scatter_ref,
     qd8_ref, kd8_ref, v8_ref, o8_ref) = refs[10 + n_alias:]
    step = pl.program_id(0)
    tok = lambda t: slice(t * DEC_BATCH, (t + 1) * DEC_BATCH)
    rec = lambda part: [_rec_col(h, part) for h in range(N_HEADS)]

    def heads(rows, part):
        return jnp.concatenate([z_ref[rows, c] for c in rec(part)], axis=1)

    @pl.when(step == 0)
    def _():
        hn = _rmsnorm(x_ref[...], n1_ref[...]).astype(BF16)
        z_ref[...] = _dot(hn, win_ref[...])

        def ext(j, lanes):
            if j >= POOL_BUF:
                return z_ref[tok(j - POOL_BUF), lanes]
            return sp_ref[:, j * D_POOL + lanes.start:j * D_POOL + lanes.stop]

        for gi, w in enumerate(POOL_WINDOWS):
            lanes = slice(gi * POOL_GROUP, (gi + 1) * POOL_GROUP)
            ms = []
            for t in range(DEC_SEQ):
                acc = ext(POOL_BUF + t, lanes)
                u = acc
                for j in range(1, w):
                    acc = acc + ext(POOL_BUF + t - j, lanes)
                ms.append(acc * (1.0 / w) - u)
            y = _dot(jnp.concatenate(ms, axis=0).astype(BF16), wpool_ref[gi]) * pscale_ref[:, lanes]
            mix_ref[:, lanes] = y
        full = slice(0, D_POOL)
        for j in range(POOL_BUF):
            spo_ref[:, j * D_POOL:(j + 1) * D_POOL] = ext(j + DEC_SEQ, full)

        lb = _lower_bound(lbraw_ref[...], layer)
        qs, ks, gs, vs = [], [], [], []
        for t in range(DEC_SEQ):
            qs.append(_silu(heads(tok(t), 0)) * Q_SCALE)
            log_f, k = _forget_gate(heads(tok(t), 1), lb)
            ks.append(k)
            gs.append(log_f if t == 0 else gs[-1] + log_f)
            vs.append(heads(tok(t), 2))
        g_last = gs[-1]
        for t in range(DEC_SEQ):
            acc = None
            for s in range(t + 1):
                prod = qs[t] * ks[s]
                if s < t:
                    prod = prod * jnp.exp(gs[t] - gs[s])
                parts = []
                for h in range(N_HEADS):
                    hl = slice(h * HEAD, (h + 1) * HEAD)
                    a = jnp.sum(prod[:, hl], axis=-1, keepdims=True)
                    parts.append(a * vs[s][:, hl])
                term = jnp.concatenate(parts, axis=1)
                acc = term if acc is None else acc + term
            oin_ref[tok(t), :] = acc

        row = lax.broadcasted_iota(jnp.int32, (S_SLOT, N_SRC * DEC_BATCH), 0)
        col = lax.broadcasted_iota(jnp.int32, (S_SLOT, N_SRC * DEC_BATCH), 1)
        hit = ((row & (SUBLANES - 1)) == (col >> 7)) & ((row >> 3) == (col & (DEC_BATCH - 1)))
        gather_ref[...] = jnp.where(hit, 1.0, 0.0).astype(BF16)
        row = lax.broadcasted_iota(jnp.int32, (S_ROWS, S_SLOT), 0)
        col = lax.broadcasted_iota(jnp.int32, (S_ROWS, S_SLOT), 1)
        hit = ((col & (SUBLANES - 1)) == (row >> 7)) & ((col >> 3) == (row & (DEC_BATCH - 1)))
        scatter_ref[...] = jnp.where(hit, 1.0, 0.0).astype(BF16)

        for t in range(DEC_SEQ):
            src_ref[tok(t), :] = (ks[t] * jnp.exp(g_last - gs[t])).astype(BF16)
        rest = jnp.exp(g_last)
        for part in range(3):
            piece = rest.astype(BF16)
            src_ref[tok(DEC_SEQ + part), :] = piece
            rest = rest - piece.astype(F32)
        kd8_ref[...] = _dot(gather_ref[...], src_ref[...])
        for t in range(DEC_SEQ):
            src_ref[tok(t), :] = (qs[t] * jnp.exp(gs[t])).astype(BF16)
        qd8_ref[...] = _dot(gather_ref[:, 0:S_ROWS], src_ref[0:S_ROWS, :])
        for t in range(DEC_SEQ):
            src_ref[tok(t), :] = vs[t].astype(BF16)
        v8_ref[...] = _dot(gather_ref[:, 0:S_ROWS], src_ref[0:S_ROWS, :])

    slot = lax.broadcasted_iota(jnp.int32, (SUBLANES, HEAD), 0)
    ones8 = jnp.where((slot >= DEC_SEQ) & (slot < N_SRC), 1.0, 0.0).astype(BF16)
    base = pl.multiple_of(step * (SEQ_BLOCK * SUBLANES), SEQ_BLOCK * SUBLANES)
    for j in range(SEQ_BLOCK):
        rows = pl.ds(base + j * SUBLANES, SUBLANES)
        for h in range(N_HEADS):
            hl = slice(h * HEAD, (h + 1) * HEAD)
            st = s_ref[j, h]
            kd = kd8_ref[rows, hl].astype(BF16)
            o8_ref[rows, hl] = _dot(qd8_ref[rows, hl].astype(BF16), st.astype(BF16))
            so_ref[j, h] = _dot_tn(kd, ones8) * st + _dot_tn(kd, v8_ref[rows, hl].astype(BF16))

    @pl.when(step == pl.num_programs(0) - 1)
    def _():
        o8 = o8_ref[...]
        hi = o8.astype(BF16)
        lo = (o8 - hi.astype(F32)).astype(BF16)
        o = oin_ref[...] + _dot(scatter_ref[...], hi) + _dot(scatter_ref[...], lo)
        for h in range(N_HEADS):
            hl = slice(h * HEAD, (h + 1) * HEAD)
            oh = o[:, hl]
            oh = oh * lax.rsqrt(jnp.mean(oh * oh, axis=-1, keepdims=True) + EPS) * gnorm_ref[...]
            mix_ref[:, D_POOL + h * HEAD:D_POOL + (h + 1) * HEAD] = oh * _silu(z_ref[:, _rec_col(h, 3)])
        xo_ref[...] = x_ref[...] + _dot(mix_ref[...].astype(BF16), wout_ref[...])


def _sample_mixer(x, p, layer, state_pool, state_rec, acc, *, name):
    alias_specs, alias_args, n_alias = _stacked(acc)
    n_in = 10
    kern = functools.partial(_sample_mixer_kernel, n_alias=n_alias, layer=layer)
    rec_block = (None, SEQ_BLOCK, N_HEADS, HEAD, HEAD)
    return pl.pallas_call(
        kern,
        grid=(DEC_BATCH // SEQ_BLOCK,),
        in_specs=[
            _const_spec((S_ROWS, D_MODEL)),
            _layer_spec((1, D_MODEL), layer),
            _layer_spec((D_MODEL, D_IN), layer),
            _layer_spec((len(POOL_WINDOWS), POOL_GROUP, POOL_GROUP), layer),
            _layer_spec((1, D_POOL), layer),
            _const_spec((DEPTH, D_REC)),
            _layer_spec((1, HEAD), layer),
            _layer_spec((D_MODEL, D_MODEL), layer),
            _layer_spec((DEC_BATCH, POOL_BUF * D_POOL), layer),
            pl.BlockSpec(rec_block, lambda s: (layer, s, 0, 0, 0)),
        ] + alias_specs,
        out_specs=[
            pl.BlockSpec((S_ROWS, D_MODEL), lambda s: (0, 0)),
            pl.BlockSpec((None, DEC_BATCH, POOL_BUF * D_POOL), lambda s: (layer, 0, 0)),
            pl.BlockSpec(rec_block, lambda s: (layer, s, 0, 0, 0)),
        ],
        out_shape=[
            jax.ShapeDtypeStruct((S_ROWS, D_MODEL), F32),
            jax.ShapeDtypeStruct((DEPTH, DEC_BATCH, POOL_BUF * D_POOL), F32),
            jax.ShapeDtypeStruct((DEPTH, DEC_BATCH, N_HEADS, HEAD, HEAD), F32),
        ],
        scratch_shapes=[
            pltpu.VMEM((S_ROWS, D_IN), F32),
            pltpu.VMEM((S_ROWS, D_MODEL), F32),
            pltpu.VMEM((S_ROWS, D_REC), F32),
            pltpu.VMEM((N_SRC * DEC_BATCH, D_REC), BF16),
            pltpu.VMEM((S_SLOT, N_SRC * DEC_BATCH), BF16),
            pltpu.VMEM((S_ROWS, S_SLOT), BF16),
            pltpu.VMEM((S_SLOT, D_REC), F32),
            pltpu.VMEM((S_SLOT, D_REC), F32),
            pltpu.VMEM((S_SLOT, D_REC), F32),
            pltpu.VMEM((S_SLOT, D_REC), F32),
        ],
        input_output_aliases={n_in + k: 1 + k for k in range(n_alias)},
        compiler_params=pltpu.CompilerParams(
            dimension_semantics=("arbitrary",), vmem_limit_bytes=VMEM_LIMIT),
        name=name,
    )(x, p["n1"], p["w_in"], p["w_pool"], p["pool_scale"], p["lb_raw"], p["g_norm"], p["w_out"],
      state_pool, state_rec, *alias_args)


def _sample_ffn_kernel(*refs, n_alias, final):
    x_ref, n2_ref, wa_ref, wb_ref, cw_ref, cb_ref, wd_ref, cs_ref, fg_ref = refs[:9]
    xo_ref, cso_ref = refs[9 + n_alias:]
    tok = lambda t: slice(t * DEC_BATCH, (t + 1) * DEC_BATCH)
    x = x_ref[...]
    hn = _rmsnorm(x, n2_ref[...]).astype(BF16)
    a = _dot(hn, wa_ref[...])
    b = _dot(hn, wb_ref[...])
    ext = [cs_ref[:, j * D_FF:(j + 1) * D_FF] for j in range(CONV_W - 1)] + [a[tok(t)] for t in range(DEC_SEQ)]
    acs = []
    for t in range(DEC_SEQ):
        ac = cb_ref[...]
        for j in range(CONV_W):
            ac = ac + cw_ref[j:j + 1, :] * ext[t + j]
        acs.append(ac)
    act = (_gelu(jnp.concatenate(acs, axis=0)) * b).astype(BF16)
    xo = x + _dot(act, wd_ref[...])
    if final:
        xo = _rmsnorm(xo, fg_ref[...])
    xo_ref[...] = xo
    for j in range(CONV_W - 1):
        cso_ref[:, j * D_FF:(j + 1) * D_FF] = ext[DEC_SEQ + j]


def _sample_ffn(x, p, layer, state_conv, acc, *, final, name):
    alias_specs, alias_args, n_alias = _stacked(acc)
    n_in = 9
    kern = functools.partial(_sample_ffn_kernel, n_alias=n_alias, final=final)
    return pl.pallas_call(
        kern,
        grid=(1,),
        in_specs=[
            _const_spec((S_ROWS, D_MODEL)),
            _layer_spec((1, D_MODEL), layer),
            _layer_spec((D_MODEL, D_FF), layer),
            _layer_spec((D_MODEL, D_FF), layer),
            _layer_spec((CONV_W, D_FF), layer),
            _layer_spec((1, D_FF), layer),
            _layer_spec((D_FF, D_MODEL), layer),
            _layer_spec((DEC_BATCH, (CONV_W - 1) * D_FF), layer),
            _const_spec((1, D_MODEL)),
        ] + alias_specs,
        out_specs=[
            pl.BlockSpec((S_ROWS, D_MODEL), lambda s: (0, 0)),
            pl.BlockSpec((None, DEC_BATCH, (CONV_W - 1) * D_FF), lambda s: (layer, 0, 0)),
        ],
        out_shape=[
            jax.ShapeDtypeStruct((S_ROWS, D_MODEL), F32),
            jax.ShapeDtypeStruct((DEPTH, DEC_BATCH, (CONV_W - 1) * D_FF), F32),
        ],
        input_output_aliases={n_in + k: 1 + k for k in range(n_alias)},
        compiler_params=pltpu.CompilerParams(
            dimension_semantics=("arbitrary",), vmem_limit_bytes=VMEM_LIMIT),
        name=name,
    )(x, p["n2"], p["w_a"], p["w_b"], p["conv_w"], p["conv_b"], p["w_down"], state_conv, p["final_g"],
      *alias_args)


PROMPT_TILE = 512
PROMPT_SUB = 128
PROMPT_CHUNK = 64


def _head_contiguous(w_in):
    cols = [w_in[..., :D_POOL]]
    for h in range(N_HEADS):
        for part in range(N_PARTS):
            lo = D_POOL + part * D_REC + h * HEAD
            cols.append(w_in[..., lo:lo + HEAD])
    return jnp.concatenate(cols, axis=-1)


def kernel(x_prompt, x_sample, state_pool, state_rec, state_conv, meta_tokens, norm1_g, w_in, w_pool, pool_scale, lb_raw, rec_norm_g, w_out, norm2_g, w_a, w_b, conv_w, conv_b, w_down, final_g):
    p = dict(
        n1=norm1_g.reshape(DEPTH, 1, D_MODEL), w_in=_head_contiguous(w_in).astype(BF16),
        w_pool=w_pool.astype(BF16), pool_scale=pool_scale.reshape(DEPTH, 1, D_POOL), lb_raw=lb_raw,
        g_norm=rec_norm_g.reshape(DEPTH, 1, HEAD), w_out=w_out.astype(BF16),
        n2=norm2_g.reshape(DEPTH, 1, D_MODEL), w_a=w_a.astype(BF16), w_b=w_b.astype(BF16),
        conv_w=conv_w, conv_b=conv_b.reshape(DEPTH, 1, D_FF), w_down=w_down.astype(BF16),
        final_g=final_g.reshape(1, D_MODEL),
    )
    last = DEPTH - 1

    xm = meta_tokens.reshape(1, N_META, D_MODEL)
    zero_pool = jnp.zeros((HIST, D_POOL), F32)
    zero_rec = jnp.zeros((N_HEADS, HEAD, HEAD), F32)
    zero_conv = jnp.zeros((CONV_W - 1, D_FF), F32)
    mix_acc = ffn_acc = None
    for l in range(DEPTH):
        xm, *mix_acc = _seq_mixer(xm, p, l, zero_pool, zero_rec, mix_acc, tile=N_META, sub=N_META,
                                  chunk=N_META, start=0, name=f"meta_mixer_{l}")
        xm, *ffn_acc = _seq_ffn(xm, p, l, zero_conv, ffn_acc, tile=N_META, final=False, name=f"meta_ffn_{l}")
    pool_m, rec_m = mix_acc
    conv_m, = ffn_acc
    pool_m = jnp.pad(pool_m[:, 0], ((0, 0), (HIST - POOL_BUF, 0), (0, 0)))

    xp = x_prompt
    mix_acc = ffn_acc = None
    for l in range(DEPTH):
        xp, *mix_acc = _seq_mixer(xp, p, l, pool_m[l], rec_m[l, 0], mix_acc, tile=PROMPT_TILE, sub=PROMPT_SUB,
                                  chunk=PROMPT_CHUNK, start=N_META, name=f"prompt_mixer_{l}")
        xp, *ffn_acc = _seq_ffn(xp, p, l, conv_m[l, 0], ffn_acc, tile=PROMPT_TILE, final=(l == last),
                                name=f"prompt_ffn_{l}")
    pool_p, rec_p = mix_acc
    conv_p, = ffn_acc

    xs = x_sample.transpose(1, 0, 2).reshape(S_ROWS, D_MODEL)
    pool_flat = state_pool.reshape(DEPTH, DEC_BATCH, POOL_BUF * D_POOL)
    conv_flat = state_conv.reshape(DEPTH, DEC_BATCH, (CONV_W - 1) * D_FF)
    mix_acc = ffn_acc = None
    for l in range(DEPTH):
        xs, *mix_acc = _sample_mixer(xs, p, l, pool_flat, state_rec, mix_acc, name=f"sample_mixer_{l}")
        xs, *ffn_acc = _sample_ffn(xs, p, l, conv_flat, ffn_acc, final=(l == last), name=f"sample_ffn_{l}")
    pool_s, rec_s = mix_acc
    conv_s, = ffn_acc
    y_sample = xs.reshape(DEC_SEQ, DEC_BATCH, D_MODEL).transpose(1, 0, 2)

    return (xp, y_sample, pool_p, rec_p, conv_p,
            pool_s.reshape(DEPTH, DEC_BATCH, POOL_BUF, D_POOL), rec_s,
            conv_s.reshape(DEPTH, DEC_BATCH, CONV_W - 1, D_FF))
```

```python
import functools
import itertools

import jax
import jax.numpy as jnp
from jax import lax
from jax.experimental import pallas as pl
from jax.experimental.pallas import tpu as pltpu

F32 = jnp.float32
BF16 = jnp.bfloat16

D_MODEL = 1024
DEPTH = 4
D_POOL = 512
D_REC = 512
N_HEADS = 4
HEAD = 128
N_PARTS = 4
D_IN = D_POOL + N_PARTS * D_REC
D_FF = 2816
POOL_WINDOWS = (2, 4, 8, 16)
POOL_GROUP = 128
POOL_BUF = 15
N_META = 16
DEC_BATCH = 128
DEC_SEQ = 4
CONV_W = 3
EPS = 1e-6
Q_SCALE = HEAD ** -0.5
SQRT_HALF = 0.7071067811865476

SUBLANES = 8
MXU_COLS = 256
HIST = 16
CONV_HIST = 8
VMEM_LIMIT = 56 * 1024 * 1024


def _rmsnorm(x, g):
    return x * lax.rsqrt(jnp.mean(x * x, axis=-1, keepdims=True) + EPS) * g


def _silu(x):
    return x * (1.0 / (1.0 + jnp.exp(-x)))


def _gelu(x):
    return 0.5 * x * (1.0 + lax.erf(x * SQRT_HALF))


def _lower_bound(lb_raw, layer):
    m = jnp.max(lb_raw, axis=0, keepdims=True)
    e = jnp.exp(lb_raw - m)
    p = e / jnp.sum(e, axis=0, keepdims=True)
    lb = jnp.zeros_like(m)
    for i in range(1, layer + 1):
        lb = lb + p[i:i + 1]
    return lb


def _forget_gate(f, lb):
    m = jnp.exp(-jnp.abs(f))
    r = 1.0 / (1.0 + m)
    sig_neg = jnp.where(f >= 0, m * r, r)
    log_sig = jnp.minimum(f, 0.0) - jnp.log(1.0 + m)
    a = jnp.log(lb)
    b = jnp.log(1.0 - lb) + log_sig
    log_f = jnp.maximum(a, b) + jnp.log(1.0 + jnp.exp(-jnp.abs(a - b)))
    return log_f, (1.0 - lb) * sig_neg


def _dot(a, b):
    return jnp.dot(a, b, preferred_element_type=F32)


def _dot_nt(a, b):
    return lax.dot_general(a, b, (((1,), (1,)), ((), ())), preferred_element_type=F32)


def _dot_tn(a, b):
    return lax.dot_general(a, b, (((0,), (0,)), ((), ())), preferred_element_type=F32)


HALF = SUBLANES // 2


def _gla_masks(n_blk):
    c_len = SUBLANES * n_blk
    lane = lax.broadcasted_iota(jnp.int32, (SUBLANES, c_len), 1)
    row = lax.broadcasted_iota(jnp.int32, (SUBLANES, c_len), 0)
    diag = lane - row
    near = [[diag == SUBLANES * i - r for r in range(HALF)] for i in range(n_blk)]
    across = [(row >= HALF) & (lane >= SUBLANES * i) & (lane < SUBLANES * i + HALF) for i in range(n_blk)]
    below = {c: lane < SUBLANES * c for c in range(1, n_blk)}
    return near, across, below


def _gla_intra(q, k, g, v, n_blk, masks):
    near, across, below = masks
    sub = lax.broadcasted_iota(jnp.int32, (SUBLANES, HEAD), 0)
    blk = lambda a, i: a[SUBLANES * i:SUBLANES * (i + 1)]
    square = lambda x: _dot_nt(x, x)

    local, tot = [], []
    for i in range(n_blk):
        x = blk(g, i)
        for s in (1, 2, 4):
            x = x + jnp.where(sub >= s, pltpu.roll(x, s, 0), 0.0)
        local.append(x)
        tot.append(jnp.exp(x[SUBLANES - 1:SUBLANES]))

    spans = {}

    def span(a, b):
        if a >= b:
            return None
        if (a, b) not in spans:
            head = span(a, b - 1)
            spans[a, b] = tot[b - 1] if head is None else head * tot[b - 1]
        return spans[a, b]

    scaled = lambda x, s: x if s is None else x * s
    q_in = [blk(q, i) * jnp.exp(local[i]) for i in range(n_blk)]
    k_out = [blk(k, i) * jnp.exp(local[i][SUBLANES - 1:SUBLANES] - local[i]) for i in range(n_blk)]
    rows_bf16 = lambda tiles: jnp.concatenate(tiles, axis=0).astype(BF16)
    q_dec = rows_bf16([scaled(q_in[i], span(0, i)) for i in range(n_blk)])
    k_dec = rows_bf16([scaled(k_out[i], span(i + 1, n_blk)) for i in range(n_blk)])

    level_prod = {}
    half = n_blk // 2
    while half >= 1:
        tiles = []
        for i in range(n_blk):
            cut = (i // (2 * half)) * 2 * half + half
            tiles.append(scaled(q_in[i], span(cut, i)) if i >= cut else scaled(k_out[i], span(i + 1, cut)))
        level_prod[half] = square(rows_bf16(tiles))
        half //= 2

    tiles = []
    for i in range(n_blk):
        mid = jnp.broadcast_to(local[i][HALF - 1:HALF], (SUBLANES, HEAD))
        tiles.append(jnp.where(sub >= HALF, blk(q, i), blk(k, i)) * jnp.exp(-jnp.abs(local[i] - mid)))
    half_prod = square(rows_bf16(tiles))

    rows = []
    for i in range(n_blk):
        ki = blk(k, i)
        gate = jnp.exp(blk(g, i))
        row = jnp.where(across[i], blk(half_prod, i), 0.0)
        w = blk(q, i)
        for r in range(HALF):
            if r > 0:
                w = w * (gate if r == 1 else pltpu.roll(gate, r - 1, 0))
            a = jnp.sum(w * (ki if r == 0 else pltpu.roll(ki, r, 0)), axis=-1, keepdims=True)
            row = jnp.where(near[i][r], a, row)
        half = 1
        while half < n_blk:
            if i % (2 * half) >= half:
                row = jnp.where(below[(i // (2 * half)) * 2 * half + half], blk(level_prod[half], i), row)
            half *= 2
        rows.append(row)

    v_b = v.astype(BF16)
    return _dot(rows_bf16(rows), v_b), q_dec, k_dec, v_b, span(0, n_blk)


def _gla_state(intra, st):
    o_intra, q_dec, k_dec, v_b, decay = intra
    out = o_intra + _dot_nt(q_dec, st.astype(BF16))
    return out, st * decay + _dot_tn(v_b, k_dec)


def _pool_count(start, offset, rows, w):
    if start >= max(POOL_WINDOWS) - 1:
        return 1.0 / w
    pos = start + offset + lax.broadcasted_iota(jnp.int32, (rows, 1), 0)
    return 1.0 / jnp.minimum(pos + 1, w).astype(F32)


def _rec_col(h, part):
    lo = D_POOL + (h * N_PARTS + part) * HEAD
    return slice(lo, lo + HEAD)


def _layer_spec(shape, layer):
    zeros = (0,) * len(shape)
    return pl.BlockSpec((None,) + tuple(shape), lambda *_: (layer,) + zeros, pipeline_mode=pl.Buffered(1))


def _const_spec(shape):
    zeros = (0,) * len(shape)
    return pl.BlockSpec(tuple(shape), lambda *_: zeros, pipeline_mode=pl.Buffered(1))


def _stacked(acc):
    if acc is None:
        return [], [], 0
    return [pl.BlockSpec(memory_space=pl.ANY)] * len(acc), list(acc), len(acc)


def _seq_mixer_kernel(*refs, n_alias, layer, tile, sub, chunk, start):
    (x_ref, xn_ref, n1_ref, win_ref, wpool_ref, pscale_ref, lbraw_ref, gnorm_ref, wout_ref,
     pool0_ref, s0_ref) = refs[:11]
    xo_ref, poolo_ref, so_ref, z0_ref, z1_ref, mix0_ref, mix1_ref, u_ref, st_ref, hn_ref = refs[11 + n_alias:]
    z_refs, mix_refs = (z0_ref, z1_ref), (mix0_ref, mix1_ref)
    b, t, n_tiles = pl.program_id(0), pl.program_id(1), pl.num_programs(1)
    n_sub = tile // sub
    unroll = min(2, n_sub)
    n_blk = chunk // SUBLANES
    n_chunks = sub // chunk
    lb = _lower_bound(lbraw_ref[...], layer)
    gnorm = gnorm_ref[...]

    def project(x, slot, offset):
        z_ref, mix_ref = z_refs[slot], mix_refs[slot]

        def norm():
            hn_ref[...] = _rmsnorm(x, n1_ref[...]).astype(BF16)

        def columns(n):
            cols = slice(n * MXU_COLS, (n + 1) * MXU_COLS)

            def step():
                z_ref[:, cols] = _dot(hn_ref[...], win_ref[:, cols])
            return step

        def pool():
            u_ref[HIST:HIST + sub, :] = z_ref[:, 0:D_POOL]
            for gi, w in enumerate(POOL_WINDOWS):
                lanes = slice(gi * POOL_GROUP, (gi + 1) * POOL_GROUP)
                acc = u_ref[HIST:HIST + sub, lanes]
                u = acc
                for j in range(1, w):
                    acc = acc + u_ref[HIST - j:HIST - j + sub, lanes]
                m = acc * _pool_count(start, offset, sub, w) - u
                y = _dot(m.astype(BF16), wpool_ref[gi]) * pscale_ref[:, lanes]
                mix_ref[:, lanes] = y.astype(BF16)
            u_ref[0:HIST, :] = u_ref[sub:sub + HIST, :]

        n_pool = D_POOL // MXU_COLS
        tiles = [columns(n) for n in range(D_IN // MXU_COLS)]
        return [norm] + tiles[:n_pool] + [pool] + tiles[n_pool:]

    def attend(x, slot, r0):
        z_ref, mix_ref = z_refs[slot], mix_refs[slot]
        intra = {}
        masks = _gla_masks(n_blk)

        def within(h, c):
            rows = slice(c * chunk, (c + 1) * chunk)

            def step():
                q = _silu(z_ref[rows, _rec_col(h, 0)]) * Q_SCALE
                log_f, k = _forget_gate(z_ref[rows, _rec_col(h, 1)], lb[:, h * HEAD:(h + 1) * HEAD])
                intra[h, c] = _gla_intra(q, k, log_f, z_ref[rows, _rec_col(h, 2)], n_blk, masks)
            return step

        state = {}

        def carry(h, c):
            rows = slice(c * chunk, (c + 1) * chunk)

            def step():
                st = st_ref[h] if c == 0 else state[h]
                o, state[h] = _gla_state(intra.pop((h, c)), st)
                o = o * lax.rsqrt(jnp.mean(o * o, axis=-1, keepdims=True) + EPS) * gnorm
                o = o * _silu(z_ref[rows, _rec_col(h, 3)])
                mix_ref[rows, D_POOL + h * HEAD:D_POOL + (h + 1) * HEAD] = o.astype(BF16)
                if c == n_chunks - 1:
                    st_ref[h] = state.pop(h)
            return step

        y = {}

        def out(lo, hi, last):
            def step():
                part = _dot(mix_ref[:, lo:hi], wout_ref[lo:hi, :])
                y["acc"] = part if "acc" not in y else y["acc"] + part
                if last:
                    xo_ref[0, pl.ds(r0, sub), :] = x + y.pop("acc")
            return step

        heavy, light = [], {}
        light[0] = [out(0, D_POOL, False)]
        for h in range(N_HEADS):
            for c in range(n_chunks):
                heavy.append(within(h, c))
                light[len(heavy)] = [carry(h, c)]
            if h % 2 == 1:
                lo = D_POOL + (h - 1) * HEAD
                light[len(heavy)].append(out(lo, lo + 2 * HEAD, h == N_HEADS - 1))
        return heavy, light

    def emit(filler, heavy=(), light=None):
        light = light or {}
        n = max(len(heavy), 1)
        done = 0
        for i in range(len(heavy) + 1):
            for step in light.get(i, ()):
                step()
            want = len(filler) if i >= len(heavy) else (len(filler) * (i + 1)) // n
            while done < want:
                filler[done]()
                done += 1
            if i < len(heavy):
                heavy[i]()

    @pl.when((b == 0) & (t == 0))
    def _():
        u_ref[0:HIST, :] = pool0_ref[...]
        emit(project(x_ref[0, 0:sub, :], 0, 0))

    @pl.when(t == 0)
    def _():
        for h in range(N_HEADS):
            st_ref[h] = s0_ref[h].T

    def trip(m, carry):
        for kk in range(unroll):
            i = m * unroll + kk
            r0 = pl.multiple_of(i * sub, sub)
            x = x_ref[0, pl.ds(r0, sub), :]
            poolo_ref[0] = u_ref[HIST - POOL_BUF:HIST, :]
            if kk == unroll - 1:
                tile_end = i == n_sub - 1
                seq_end = tile_end & (t == n_tiles - 1)
                u_ref[0:HIST, :] = jnp.where(seq_end, pool0_ref[...], u_ref[0:HIST, :])
                rn = pl.multiple_of(jnp.minimum(i + 1, n_sub - 1) * sub, sub)
                x_next = jnp.where(tile_end, xn_ref[0], x_ref[0, pl.ds(rn, sub), :])
            else:
                x_next = x_ref[0, pl.ds(r0 + sub, sub), :]
            offset = lax.rem(t * n_sub + i + 1, n_tiles * n_sub) * sub
            emit(project(x_next, (kk + 1) % 2, offset), *attend(x, kk % 2, r0))
        return carry

    lax.fori_loop(0, n_sub // unroll, trip, 0)

    @pl.when(t == n_tiles - 1)
    def _():
        for h in range(N_HEADS):
            so_ref[0, h] = st_ref[h].T


def _seq_mixer(x, p, layer, pool0, s0, acc, *, tile, sub, chunk, start, name):
    n_seq, t_len, _ = x.shape
    n_tiles, n_sub = t_len // tile, tile // sub
    alias_specs, alias_args, n_alias = _stacked(acc)
    n_in = 11

    def next_rows(b, t):
        nxt = jnp.minimum(b * n_tiles + t + 1, n_seq * n_tiles - 1)
        return nxt // n_tiles, (nxt % n_tiles) * n_sub, 0

    kern = functools.partial(_seq_mixer_kernel, n_alias=n_alias, layer=layer, tile=tile, sub=sub,
                             chunk=chunk, start=start)
    return pl.pallas_call(
        kern,
        grid=(n_seq, n_tiles),
        in_specs=[
            pl.BlockSpec((1, tile, D_MODEL), lambda b, t: (b, t, 0)),
            pl.BlockSpec((1, sub, D_MODEL), next_rows),
            _layer_spec((1, D_MODEL), layer),
            _layer_spec((D_MODEL, D_IN), layer),
            _layer_spec((len(POOL_WINDOWS), POOL_GROUP, POOL_GROUP), layer),
            _layer_spec((1, D_POOL), layer),
            _const_spec((DEPTH, D_REC)),
            _layer_spec((1, HEAD), layer),
            _layer_spec((D_MODEL, D_MODEL), layer),
            _const_spec((HIST, D_POOL)),
            _const_spec((N_HEADS, HEAD, HEAD)),
        ] + alias_specs,
        out_specs=[
            pl.BlockSpec((1, tile, D_MODEL), lambda b, t: (b, t, 0)),
            pl.BlockSpec((None, 1, POOL_BUF, D_POOL), lambda b, t: (layer, b, 0, 0)),
            pl.BlockSpec((None, 1, N_HEADS, HEAD, HEAD), lambda b, t: (layer, b, 0, 0, 0)),
        ],
        out_shape=[
            jax.ShapeDtypeStruct((n_seq, t_len, D_MODEL), F32),
            jax.ShapeDtypeStruct((DEPTH, n_seq, POOL_BUF, D_POOL), F32),
            jax.ShapeDtypeStruct((DEPTH, n_seq, N_HEADS, HEAD, HEAD), F32),
        ],
        scratch_shapes=[
            pltpu.VMEM((sub, D_IN), F32),
            pltpu.VMEM((sub, D_IN), F32),
            pltpu.VMEM((sub, D_MODEL), BF16),
            pltpu.VMEM((sub, D_MODEL), BF16),
            pltpu.VMEM((HIST + sub, D_POOL), F32),
            pltpu.VMEM((N_HEADS, HEAD, HEAD), F32),
            pltpu.VMEM((sub, D_MODEL), BF16),
        ],
        input_output_aliases={n_in + k: 1 + k for k in range(n_alias)},
        compiler_params=pltpu.CompilerParams(
            dimension_semantics=("arbitrary", "arbitrary"), vmem_limit_bytes=VMEM_LIMIT),
        name=name,
    )(x, x, p["n1"], p["w_in"], p["w_pool"], p["pool_scale"], p["lb_raw"], p["g_norm"], p["w_out"], pool0, s0,
      *alias_args)


def _seq_ffn_kernel(*refs, n_alias, tile, final):
    x_ref, n2_ref, wa_ref, wb_ref, cw_ref, cb_ref, wd_ref, conv0_ref, fg_ref = refs[:9]
    xo_ref, convo_ref, a_ref = refs[9 + n_alias:]
    t = pl.program_id(1)
    lo = CONV_HIST - (CONV_W - 1)

    @pl.when(t == 0)
    def _():
        a_ref[lo:CONV_HIST, :] = conv0_ref[...]

    x = x_ref[0]
    hn = _rmsnorm(x, n2_ref[...]).astype(BF16)
    a_ref[CONV_HIST:CONV_HIST + tile, :] = _dot(hn, wa_ref[...])
    b = _dot(hn, wb_ref[...])
    ac = cb_ref[...]
    for j in range(CONV_W):
        ac = ac + cw_ref[j:j + 1, :] * a_ref[lo + j:lo + j + tile, :]
    act = (_gelu(ac) * b).astype(BF16)
    xo = x + _dot(act, wd_ref[...])
    if final:
        xo = _rmsnorm(xo, fg_ref[...])
    xo_ref[0] = xo

    @pl.when(t == pl.num_programs(1) - 1)
    def _():
        convo_ref[0] = a_ref[lo + tile:CONV_HIST + tile, :]

    a_ref[lo:CONV_HIST, :] = a_ref[lo + tile:CONV_HIST + tile, :]


def _seq_ffn(x, p, layer, conv0, acc, *, tile, final, name):
    n_seq, t_len, _ = x.shape
    alias_specs, alias_args, n_alias = _stacked(acc)
    n_in = 9
    kern = functools.partial(_seq_ffn_kernel, n_alias=n_alias, tile=tile, final=final)
    return pl.pallas_call(
        kern,
        grid=(n_seq, t_len // tile),
        in_specs=[
            pl.BlockSpec((1, tile, D_MODEL), lambda b, t: (b, t, 0)),
            _layer_spec((1, D_MODEL), layer),
            _layer_spec((D_MODEL, D_FF), layer),
            _layer_spec((D_MODEL, D_FF), layer),
            _layer_spec((CONV_W, D_FF), layer),
            _layer_spec((1, D_FF), layer),
            _layer_spec((D_FF, D_MODEL), layer),
            _const_spec((CONV_W - 1, D_FF)),
            _const_spec((1, D_MODEL)),
        ] + alias_specs,
        out_specs=[
            pl.BlockSpec((1, tile, D_MODEL), lambda b, t: (b, t, 0)),
            pl.BlockSpec((None, 1, CONV_W - 1, D_FF), lambda b, t: (layer, b, 0, 0)),
        ],
        out_shape=[
            jax.ShapeDtypeStruct((n_seq, t_len, D_MODEL), F32),
            jax.ShapeDtypeStruct((DEPTH, n_seq, CONV_W - 1, D_FF), F32),
        ],
        scratch_shapes=[pltpu.VMEM((CONV_HIST + tile, D_FF), F32)],
        input_output_aliases={n_in + k: 1 + k for k in range(n_alias)},
        compiler_params=pltpu.CompilerParams(
            dimension_semantics=("arbitrary", "arbitrary"), vmem_limit_bytes=VMEM_LIMIT),
        name=name,
    )(x, p["n2"], p["w_a"], p["w_b"], p["conv_w"], p["conv_b"], p["w_down"], conv0, p["final_g"], *alias_args)


S_ROWS = DEC_BATCH * DEC_SEQ
S_SLOT = SUBLANES * DEC_BATCH
N_SRC = DEC_SEQ + 3
SEQ_BLOCK = 8


def _sample_mixer_kernel(*refs, n_alias, layer):
    (x_ref, n1_ref, win_ref, wpool_ref, pscale_ref, lbraw_ref, gnorm_ref, wout_ref, sp_ref, s_ref) = refs[:10]
    (xo_ref, spo_ref, so_ref, z_ref, mix_ref, oin_ref, src_ref, gather_ref, scatter_ref,
     qd8_ref, kd8_ref, v8_ref, o8_ref) = refs[10 + n_alias:]
    step = pl.program_id(0)
    tok = lambda t: slice(t * DEC_BATCH, (t + 1) * DEC_BATCH)
    rec = lambda part: [_rec_col(h, part) for h in range(N_HEADS)]

    def heads(rows, part):
        return jnp.concatenate([z_ref[rows, c] for c in rec(part)], axis=1)

    @pl.when(step == 0)
    def _():
        hn = _rmsnorm(x_ref[...], n1_ref[...]).astype(BF16)
        z_ref[...] = _dot(hn, win_ref[...])

        def ext(j, lanes):
            if j >= POOL_BUF:
                return z_ref[tok(j - POOL_BUF), lanes]
            return sp_ref[:, j * D_POOL + lanes.start:j * D_POOL + lanes.stop]

        for gi, w in enumerate(POOL_WINDOWS):
            lanes = slice(gi * POOL_GROUP, (gi + 1) * POOL_GROUP)
            ms = []
            for t in range(DEC_SEQ):
                acc = ext(POOL_BUF + t, lanes)
                u = acc
                for j in range(1, w):
                    acc = acc + ext(POOL_BUF + t - j, lanes)
                ms.append(acc * (1.0 / w) - u)
            y = _dot(jnp.concatenate(ms, axis=0).astype(BF16), wpool_ref[gi]) * pscale_ref[:, lanes]
            mix_ref[:, lanes] = y
        full = slice(0, D_POOL)
        for j in range(POOL_BUF):
            spo_ref[:, j * D_POOL:(j + 1) * D_POOL] = ext(j + DEC_SEQ, full)

        lb = _lower_bound(lbraw_ref[...], layer)
        qs, ks, gs, vs = [], [], [], []
        for t in range(DEC_SEQ):
            qs.append(_silu(heads(tok(t), 0)) * Q_SCALE)
            log_f, k = _forget_gate(heads(tok(t), 1), lb)
            ks.append(k)
            gs.append(log_f if t == 0 else gs[-1] + log_f)
            vs.append(heads(tok(t), 2))
        g_last = gs[-1]
        for t in range(DEC_SEQ):
            acc = None
            for s in range(t + 1):
                prod = qs[t] * ks[s]
                if s < t:
                    prod = prod * jnp.exp(gs[t] - gs[s])
                parts = []
                for h in range(N_HEADS):
                    hl = slice(h * HEAD, (h + 1) * HEAD)
                    a = jnp.sum(prod[:, hl], axis=-1, keepdims=True)
                    parts.append(a * vs[s][:, hl])
                term = jnp.concatenate(parts, axis=1)
                acc = term if acc is None else acc + term
            oin_ref[tok(t), :] = acc

        row = lax.broadcasted_iota(jnp.int32, (S_SLOT, N_SRC * DEC_BATCH), 0)
        col = lax.broadcasted_iota(jnp.int32, (S_SLOT, N_SRC * DEC_BATCH), 1)
        hit = ((row & (SUBLANES - 1)) == (col >> 7)) & ((row >> 3) == (col & (DEC_BATCH - 1)))
        gather_ref[...] = jnp.where(hit, 1.0, 0.0).astype(BF16)
        row = lax.broadcasted_iota(jnp.int32, (S_ROWS, S_SLOT), 0)
        col = lax.broadcasted_iota(jnp.int32, (S_ROWS, S_SLOT), 1)
        hit = ((col & (SUBLANES - 1)) == (row >> 7)) & ((col >> 3) == (row & (DEC_BATCH - 1)))
        scatter_ref[...] = jnp.where(hit, 1.0, 0.0).astype(BF16)

        for t in range(DEC_SEQ):
            src_ref[tok(t), :] = (ks[t] * jnp.exp(g_last - gs[t])).astype(BF16)
        rest = jnp.exp(g_last)
        for part in range(3):
            piece = rest.astype(BF16)
            src_ref[tok(DEC_SEQ + part), :] = piece
            rest = rest - piece.astype(F32)
        kd8_ref[...] = _dot(gather_ref[...], src_ref[...])
        for t in range(DEC_SEQ):
            src_ref[tok(t), :] = (qs[t] * jnp.exp(gs[t])).astype(BF16)
        qd8_ref[...] = _dot(gather_ref[:, 0:S_ROWS], src_ref[0:S_ROWS, :])
        for t in range(DEC_SEQ):
            src_ref[tok(t), :] = vs[t].astype(BF16)
        v8_ref[...] = _dot(gather_ref[:, 0:S_ROWS], src_ref[0:S_ROWS, :])

    slot = lax.broadcasted_iota(jnp.int32, (SUBLANES, HEAD), 0)
    ones8 = jnp.where((slot >= DEC_SEQ) & (slot < N_SRC), 1.0, 0.0).astype(BF16)
    base = pl.multiple_of(step * (SEQ_BLOCK * SUBLANES), SEQ_BLOCK * SUBLANES)
    for j in range(SEQ_BLOCK):
        rows = pl.ds(base + j * SUBLANES, SUBLANES)
        for h in range(N_HEADS):
            hl = slice(h * HEAD, (h + 1) * HEAD)
            st = s_ref[j, h]
            kd = kd8_ref[rows, hl].astype(BF16)
            o8_ref[rows, hl] = _dot(qd8_ref[rows, hl].astype(BF16), st.astype(BF16))
            so_ref[j, h] = _dot_tn(kd, ones8) * st + _dot_tn(kd, v8_ref[rows, hl].astype(BF16))

    @pl.when(step == pl.num_programs(0) - 1)
    def _():
        o8 = o8_ref[...]
        hi = o8.astype(BF16)
        lo = (o8 - hi.astype(F32)).astype(BF16)
        o = oin_ref[...] + _dot(scatter_ref[...], hi) + _dot(scatter_ref[...], lo)
        for h in range(N_HEADS):
            hl = slice(h * HEAD, (h + 1) * HEAD)
            oh = o[:, hl]
            oh = oh * lax.rsqrt(jnp.mean(oh * oh, axis=-1, keepdims=True) + EPS) * gnorm_ref[...]
            mix_ref[:, D_POOL + h * HEAD:D_POOL + (h + 1) * HEAD] = oh * _silu(z_ref[:, _rec_col(h, 3)])
        xo_ref[...] = x_ref[...] + _dot(mix_ref[...].astype(BF16), wout_ref[...])


def _sample_mixer(x, p, layer, state_pool, state_rec, acc, *, name):
    alias_specs, alias_args, n_alias = _stacked(acc)
    n_in = 10
    kern = functools.partial(_sample_mixer_kernel, n_alias=n_alias, layer=layer)
    rec_block = (None, SEQ_BLOCK, N_HEADS, HEAD, HEAD)
    return pl.pallas_call(
        kern,
        grid=(DEC_BATCH // SEQ_BLOCK,),
        in_specs=[
            _const_spec((S_ROWS, D_MODEL)),
            _layer_spec((1, D_MODEL), layer),
            _layer_spec((D_MODEL, D_IN), layer),
            _layer_spec((len(POOL_WINDOWS), POOL_GROUP, POOL_GROUP), layer),
            _layer_spec((1, D_POOL), layer),
            _const_spec((DEPTH, D_REC)),
            _layer_spec((1, HEAD), layer),
            _layer_spec((D_MODEL, D_MODEL), layer),
            _layer_spec((DEC_BATCH, POOL_BUF * D_POOL), layer),
            pl.BlockSpec(rec_block, lambda s: (layer, s, 0, 0, 0)),
        ] + alias_specs,
        out_specs=[
            pl.BlockSpec((S_ROWS, D_MODEL), lambda s: (0, 0)),
            pl.BlockSpec((None, DEC_BATCH, POOL_BUF * D_POOL), lambda s: (layer, 0, 0)),
            pl.BlockSpec(rec_block, lambda s: (layer, s, 0, 0, 0)),
        ],
        out_shape=[
            jax.ShapeDtypeStruct((S_ROWS, D_MODEL), F32),
            jax.ShapeDtypeStruct((DEPTH, DEC_BATCH, POOL_BUF * D_POOL), F32),
            jax.ShapeDtypeStruct((DEPTH, DEC_BATCH, N_HEADS, HEAD, HEAD), F32),
        ],
        scratch_shapes=[
            pltpu.VMEM((S_ROWS, D_IN), F32),
            pltpu.VMEM((S_ROWS, D_MODEL), F32),
            pltpu.VMEM((S_ROWS, D_REC), F32),
            pltpu.VMEM((N_SRC * DEC_BATCH, D_REC), BF16),
            pltpu.VMEM((S_SLOT, N_SRC * DEC_BATCH), BF16),
            pltpu.VMEM((S_ROWS, S_SLOT), BF16),
            pltpu.VMEM((S_SLOT, D_REC), F32),
            pltpu.VMEM((S_SLOT, D_REC), F32),
            pltpu.VMEM((S_SLOT, D_REC), F32),
            pltpu.VMEM((S_SLOT, D_REC), F32),
        ],
        input_output_aliases={n_in + k: 1 + k for k in range(n_alias)},
        compiler_params=pltpu.CompilerParams(
            dimension_semantics=("arbitrary",), vmem_limit_bytes=VMEM_LIMIT),
        name=name,
    )(x, p["n1"], p["w_in"], p["w_pool"], p["pool_scale"], p["lb_raw"], p["g_norm"], p["w_out"],
      state_pool, state_rec, *alias_args)


def _sample_ffn_kernel(*refs, n_alias, final):
    x_ref, n2_ref, wa_ref, wb_ref, cw_ref, cb_ref, wd_ref, cs_ref, fg_ref = refs[:9]
    xo_ref, cso_ref = refs[9 + n_alias:]
    tok = lambda t: slice(t * DEC_BATCH, (t + 1) * DEC_BATCH)
    x = x_ref[...]
    hn = _rmsnorm(x, n2_ref[...]).astype(BF16)
    a = _dot(hn, wa_ref[...])
    b = _dot(hn, wb_ref[...])
    ext = [cs_ref[:, j * D_FF:(j + 1) * D_FF] for j in range(CONV_W - 1)] + [a[tok(t)] for t in range(DEC_SEQ)]
    acs = []
    for t in range(DEC_SEQ):
        ac = cb_ref[...]
        for j in range(CONV_W):
            ac = ac + cw_ref[j:j + 1, :] * ext[t + j]
        acs.append(ac)
    act = (_gelu(jnp.concatenate(acs, axis=0)) * b).astype(BF16)
    xo = x + _dot(act, wd_ref[...])
    if final:
        xo = _rmsnorm(xo, fg_ref[...])
    xo_ref[...] = xo
    for j in range(CONV_W - 1):
        cso_ref[:, j * D_FF:(j + 1) * D_FF] = ext[DEC_SEQ + j]


def _sample_ffn(x, p, layer, state_conv, acc, *, final, name):
    alias_specs, alias_args, n_alias = _stacked(acc)
    n_in = 9
    kern = functools.partial(_sample_ffn_kernel, n_alias=n_alias, final=final)
    return pl.pallas_call(
        kern,
        grid=(1,),
        in_specs=[
            _const_spec((S_ROWS, D_MODEL)),
            _layer_spec((1, D_MODEL), layer),
            _layer_spec((D_MODEL, D_FF), layer),
            _layer_spec((D_MODEL, D_FF), layer),
            _layer_spec((CONV_W, D_FF), layer),
            _layer_spec((1, D_FF), layer),
            _layer_spec((D_FF, D_MODEL), layer),
            _layer_spec((DEC_BATCH, (CONV_W - 1) * D_FF), layer),
            _const_spec((1, D_MODEL)),
        ] + alias_specs,
        out_specs=[
            pl.BlockSpec((S_ROWS, D_MODEL), lambda s: (0, 0)),
            pl.BlockSpec((None, DEC_BATCH, (CONV_W - 1) * D_FF), lambda s: (layer, 0, 0)),
        ],
        out_shape=[
            jax.ShapeDtypeStruct((S_ROWS, D_MODEL), F32),
            jax.ShapeDtypeStruct((DEPTH, DEC_BATCH, (CONV_W - 1) * D_FF), F32),
        ],
        input_output_aliases={n_in + k: 1 + k for k in range(n_alias)},
        compiler_params=pltpu.CompilerParams(
            dimension_semantics=("arbitrary",), vmem_limit_bytes=VMEM_LIMIT),
        name=name,
    )(x, p["n2"], p["w_a"], p["w_b"], p["conv_w"], p["conv_b"], p["w_down"], state_conv, p["final_g"],
      *alias_args)


PROMPT_TILE = 512
PROMPT_SUB = 256
PROMPT_CHUNK = 64


def _head_contiguous(w_in):
    cols = [w_in[..., :D_POOL]]
    for h in range(N_HEADS):
        for part in range(N_PARTS):
            lo = D_POOL + part * D_REC + h * HEAD
            cols.append(w_in[..., lo:lo + HEAD])
    return jnp.concatenate(cols, axis=-1)


def kernel(x_prompt, x_sample, state_pool, state_rec, state_conv, meta_tokens, norm1_g, w_in, w_pool, pool_scale, lb_raw, rec_norm_g, w_out, norm2_g, w_a, w_b, conv_w, conv_b, w_down, final_g):
    p = dict(
        n1=norm1_g.reshape(DEPTH, 1, D_MODEL), w_in=_head_contiguous(w_in).astype(BF16),
        w_pool=w_pool.astype(BF16), pool_scale=pool_scale.reshape(DEPTH, 1, D_POOL), lb_raw=lb_raw,
        g_norm=rec_norm_g.reshape(DEPTH, 1, HEAD), w_out=w_out.astype(BF16),
        n2=norm2_g.reshape(DEPTH, 1, D_MODEL), w_a=w_a.astype(BF16), w_b=w_b.astype(BF16),
        conv_w=conv_w, conv_b=conv_b.reshape(DEPTH, 1, D_FF), w_down=w_down.astype(BF16),
        final_g=final_g.reshape(1, D_MODEL),
    )
    last = DEPTH - 1

    xm = meta_tokens.reshape(1, N_META, D_MODEL)
    zero_pool = jnp.zeros((HIST, D_POOL), F32)
    zero_rec = jnp.zeros((N_HEADS, HEAD, HEAD), F32)
    zero_conv = jnp.zeros((CONV_W - 1, D_FF), F32)
    mix_acc = ffn_acc = None
    for l in range(DEPTH):
        xm, *mix_acc = _seq_mixer(xm, p, l, zero_pool, zero_rec, mix_acc, tile=N_META, sub=N_META,
                                  chunk=N_META, start=0, name=f"meta_mixer_{l}")
        xm, *ffn_acc = _seq_ffn(xm, p, l, zero_conv, ffn_acc, tile=N_META, final=False, name=f"meta_ffn_{l}")
    pool_m, rec_m = mix_acc
    conv_m, = ffn_acc
    pool_m = jnp.pad(pool_m[:, 0], ((0, 0), (HIST - POOL_BUF, 0), (0, 0)))

    xp = x_prompt
    mix_acc = ffn_acc = None
    for l in range(DEPTH):
        xp, *mix_acc = _seq_mixer(xp, p, l, pool_m[l], rec_m[l, 0], mix_acc, tile=PROMPT_TILE, sub=PROMPT_SUB,
                                  chunk=PROMPT_CHUNK, start=N_META, name=f"prompt_mixer_{l}")
        xp, *ffn_acc = _seq_ffn(xp, p, l, conv_m[l, 0], ffn_acc, tile=PROMPT_TILE, final=(l == last),
                                name=f"prompt_ffn_{l}")
    pool_p, rec_p = mix_acc
    conv_p, = ffn_acc

    xs = x_sample.transpose(1, 0, 2).reshape(S_ROWS, D_MODEL)
    pool_flat = state_pool.reshape(DEPTH, DEC_BATCH, POOL_BUF * D_POOL)
    conv_flat = state_conv.reshape(DEPTH, DEC_BATCH, (CONV_W - 1) * D_FF)
    mix_acc = ffn_acc = None
    for l in range(DEPTH):
        xs, *mix_acc = _sample_mixer(xs, p, l, pool_flat, state_rec, mix_acc, name=f"sample_mixer_{l}")
        xs, *ffn_acc = _sample_ffn(xs, p, l, conv_flat, ffn_acc, final=(l == last), name=f"sample_ffn_{l}")
    pool_s, rec_s = mix_acc
    conv_s, = ffn_acc
    y_sample = xs.reshape(DEC_SEQ, DEC_BATCH, D_MODEL).transpose(1, 0, 2)

    return (xp, y_sample, pool_p, rec_p, conv_p,
            pool_s.reshape(DEPTH, DEC_BATCH, POOL_BUF, D_POOL), rec_s,
            conv_s.reshape(DEPTH, DEC_BATCH, CONV_W - 1, D_FF))
```

```python
import functools
import itertools

import jax
import jax.numpy as jnp
from jax import lax
from jax.experimental import pallas as pl
from jax.experimental.pallas import tpu as pltpu

F32 = jnp.float32
BF16 = jnp.bfloat16

D_MODEL = 1024
DEPTH = 4
D_POOL = 512
D_REC = 512
N_HEADS = 4
HEAD = 128
N_PARTS = 4
D_IN = D_POOL + N_PARTS * D_REC
D_FF = 2816
POOL_WINDOWS = (2, 4, 8, 16)
POOL_GROUP = 128
POOL_BUF = 15
N_META = 16
DEC_BATCH = 128
DEC_SEQ = 4
CONV_W = 3
EPS = 1e-6
Q_SCALE = HEAD ** -0.5
SQRT_HALF = 0.7071067811865476

SUBLANES = 8
MXU_COLS = 256
HIST = 16
CONV_HIST = 8
VMEM_LIMIT = 56 * 1024 * 1024


def _rmsnorm(x, g):
    return x * lax.rsqrt(jnp.mean(x * x, axis=-1, keepdims=True) + EPS) * g


def _silu(x):
    return x * (1.0 / (1.0 + jnp.exp(-x)))


def _gelu(x):
    return 0.5 * x * (1.0 + lax.erf(x * SQRT_HALF))


def _lower_bound(lb_raw, layer):
    m = jnp.max(lb_raw, axis=0, keepdims=True)
    e = jnp.exp(lb_raw - m)
    p = e / jnp.sum(e, axis=0, keepdims=True)
    lb = jnp.zeros_like(m)
    for i in range(1, layer + 1):
        lb = lb + p[i:i + 1]
    return lb


def _forget_gate(f, lb):
    m = jnp.exp(-jnp.abs(f))
    r = 1.0 / (1.0 + m)
    sig_neg = jnp.where(f >= 0, m * r, r)
    log_sig = jnp.minimum(f, 0.0) - jnp.log(1.0 + m)
    a = jnp.log(lb)
    b = jnp.log(1.0 - lb) + log_sig
    log_f = jnp.maximum(a, b) + jnp.log(1.0 + jnp.exp(-jnp.abs(a - b)))
    return log_f, (1.0 - lb) * sig_neg


def _dot(a, b):
    return jnp.dot(a, b, preferred_element_type=F32)


def _dot_nt(a, b):
    return lax.dot_general(a, b, (((1,), (1,)), ((), ())), preferred_element_type=F32)


def _dot_tn(a, b):
    return lax.dot_general(a, b, (((0,), (0,)), ((), ())), preferred_element_type=F32)


HALF = SUBLANES // 2


def _gla_masks(n_blk):
    c_len = SUBLANES * n_blk
    lane = lax.broadcasted_iota(jnp.int32, (SUBLANES, c_len), 1)
    row = lax.broadcasted_iota(jnp.int32, (SUBLANES, c_len), 0)
    diag = lane - row
    near = [[diag == SUBLANES * i - r for r in range(HALF)] for i in range(n_blk)]
    across = [(row >= HALF) & (lane >= SUBLANES * i) & (lane < SUBLANES * i + HALF) for i in range(n_blk)]
    below = {c: lane < SUBLANES * c for c in range(1, n_blk)}
    return near, across, below


def _gla_intra(q, k, g, v, n_blk, masks):
    near, across, below = masks
    sub = lax.broadcasted_iota(jnp.int32, (SUBLANES, HEAD), 0)
    blk = lambda a, i: a[SUBLANES * i:SUBLANES * (i + 1)]
    square = lambda x: _dot_nt(x, x)

    local, tot = [], []
    for i in range(n_blk):
        x = blk(g, i)
        for s in (1, 2, 4):
            x = x + jnp.where(sub >= s, pltpu.roll(x, s, 0), 0.0)
        local.append(x)
        tot.append(jnp.exp(x[SUBLANES - 1:SUBLANES]))

    spans = {}

    def span(a, b):
        if a >= b:
            return None
        if (a, b) not in spans:
            head = span(a, b - 1)
            spans[a, b] = tot[b - 1] if head is None else head * tot[b - 1]
        return spans[a, b]

    scaled = lambda x, s: x if s is None else x * s
    q_in = [blk(q, i) * jnp.exp(local[i]) for i in range(n_blk)]
    k_out = [blk(k, i) * jnp.exp(local[i][SUBLANES - 1:SUBLANES] - local[i]) for i in range(n_blk)]
    rows_bf16 = lambda tiles: jnp.concatenate(tiles, axis=0).astype(BF16)
    q_dec = rows_bf16([scaled(q_in[i], span(0, i)) for i in range(n_blk)])
    k_dec = rows_bf16([scaled(k_out[i], span(i + 1, n_blk)) for i in range(n_blk)])

    level_prod = {}
    half = n_blk // 2
    while half >= 1:
        tiles = []
        for i in range(n_blk):
            cut = (i // (2 * half)) * 2 * half + half
            tiles.append(scaled(q_in[i], span(cut, i)) if i >= cut else scaled(k_out[i], span(i + 1, cut)))
        level_prod[half] = square(rows_bf16(tiles))
        half //= 2

    tiles = []
    for i in range(n_blk):
        mid = jnp.broadcast_to(local[i][HALF - 1:HALF], (SUBLANES, HEAD))
        tiles.append(jnp.where(sub >= HALF, blk(q, i), blk(k, i)) * jnp.exp(-jnp.abs(local[i] - mid)))
    half_prod = square(rows_bf16(tiles))

    rows = []
    for i in range(n_blk):
        ki = blk(k, i)
        gate = jnp.exp(blk(g, i))
        row = jnp.where(across[i], blk(half_prod, i), 0.0)
        w = blk(q, i)
        for r in range(HALF):
            if r > 0:
                w = w * (gate if r == 1 else pltpu.roll(gate, r - 1, 0))
            a = jnp.sum(w * (ki if r == 0 else pltpu.roll(ki, r, 0)), axis=-1, keepdims=True)
            row = jnp.where(near[i][r], a, row)
        half = 1
        while half < n_blk:
            if i % (2 * half) >= half:
                row = jnp.where(below[(i // (2 * half)) * 2 * half + half], blk(level_prod[half], i), row)
            half *= 2
        rows.append(row)

    v_b = v.astype(BF16)
    return _dot(rows_bf16(rows), v_b), q_dec, k_dec, v_b, span(0, n_blk)


def _gla_state(intra, st):
    o_intra, q_dec, k_dec, v_b, decay = intra
    out = o_intra + _dot_nt(q_dec, st.astype(BF16))
    return out, st * decay + _dot_tn(v_b, k_dec)


def _pool_count(start, offset, rows, w):
    if start >= max(POOL_WINDOWS) - 1:
        return 1.0 / w
    pos = start + offset + lax.broadcasted_iota(jnp.int32, (rows, 1), 0)
    return 1.0 / jnp.minimum(pos + 1, w).astype(F32)


def _rec_col(h, part):
    lo = D_POOL + part * D_REC + h * HEAD
    return slice(lo, lo + HEAD)


def _layer_spec(shape, layer):
    zeros = (0,) * len(shape)
    return pl.BlockSpec((None,) + tuple(shape), lambda *_: (layer,) + zeros, pipeline_mode=pl.Buffered(1))


def _const_spec(shape):
    zeros = (0,) * len(shape)
    return pl.BlockSpec(tuple(shape), lambda *_: zeros, pipeline_mode=pl.Buffered(1))


def _stacked(acc):
    return [pl.BlockSpec(memory_space=pl.ANY)] * len(acc), list(acc), len(acc)


def _prompt_mixer_kernel(*refs, n_alias, layer, tile, sub, chunk):
    (x_ref, xn_ref, xm_ref, n1_ref, win_ref, wpool_ref, pscale_ref, lbraw_ref, gnorm_ref, wout_ref) = refs[:10]
    (xo_ref, xmo_ref, poolo_ref, so_ref, z0_ref, z1_ref, mix0_ref, mix1_ref, u_ref, st_ref, hn_ref,
     pool0_ref, s0_ref) = refs[10 + n_alias:]
    z_refs, mix_refs = (z0_ref, z1_ref), (mix0_ref, mix1_ref)
    b, t, n_tiles = pl.program_id(0), pl.program_id(1), pl.num_programs(1)
    n_sub = tile // sub
    unroll = 2
    lb = _lower_bound(lbraw_ref[...], layer)
    gnorm = gnorm_ref[...]

    def project(x, slot, rows, start, offset):
        z_ref, mix_ref = z_refs[slot], mix_refs[slot]

        def norm():
            hn_ref[0:rows, :] = _rmsnorm(x, n1_ref[...]).astype(BF16)

        def columns(n):
            cols = slice(n * MXU_COLS, (n + 1) * MXU_COLS)

            def step():
                z_ref[0:rows, cols] = _dot(hn_ref[0:rows, :], win_ref[:, cols])
            return step

        def pool():
            u_ref[HIST:HIST + rows, :] = z_ref[0:rows, 0:D_POOL]
            for gi, w in enumerate(POOL_WINDOWS):
                lanes = slice(gi * POOL_GROUP, (gi + 1) * POOL_GROUP)
                acc = u_ref[HIST:HIST + rows, lanes]
                u = acc
                for j in range(1, w):
                    acc = acc + u_ref[HIST - j:HIST - j + rows, lanes]
                m = acc * _pool_count(start, offset, rows, w) - u
                y = _dot(m.astype(BF16), wpool_ref[gi]) * pscale_ref[:, lanes]
                mix_ref[0:rows, lanes] = y.astype(BF16)
            u_ref[0:HIST, :] = u_ref[rows:rows + HIST, :]

        n_pool = D_POOL // MXU_COLS
        tiles = [columns(n) for n in range(D_IN // MXU_COLS)]
        return [norm] + tiles[:n_pool] + [pool] + tiles[n_pool:]

    def attend(x, slot, rows, chunk, state_ref, store):
        z_ref, mix_ref = z_refs[slot], mix_refs[slot]
        n_blk, n_chunks = chunk // SUBLANES, rows // chunk
        intra = {}
        masks = _gla_masks(n_blk)

        def within(h, c):
            rows = slice(c * chunk, (c + 1) * chunk)

            def step():
                q = _silu(z_ref[rows, _rec_col(h, 0)]) * Q_SCALE
                log_f, k = _forget_gate(z_ref[rows, _rec_col(h, 1)], lb[:, h * HEAD:(h + 1) * HEAD])
                intra[h, c] = _gla_intra(q, k, log_f, z_ref[rows, _rec_col(h, 2)], n_blk, masks)
            return step

        state = {}

        def carry(h, c):
            rows = slice(c * chunk, (c + 1) * chunk)

            def step():
                st = state_ref[h] if c == 0 else state[h]
                o, state[h] = _gla_state(intra.pop((h, c)), st)
                o = o * lax.rsqrt(jnp.mean(o * o, axis=-1, keepdims=True) + EPS) * gnorm
                o = o * _silu(z_ref[rows, _rec_col(h, 3)])
                mix_ref[rows, D_POOL + h * HEAD:D_POOL + (h + 1) * HEAD] = o.astype(BF16)
                if c == n_chunks - 1:
                    state_ref[h] = state.pop(h)
            return step

        y = {}

        def out(lo, hi, last):
            def step():
                part = _dot(mix_ref[0:rows, lo:hi], wout_ref[lo:hi, :])
                y["acc"] = part if "acc" not in y else y["acc"] + part
                if last:
                    store(x + y.pop("acc"))
            return step

        heavy, light = [], {}
        light[0] = [out(0, D_POOL, False)]
        for h in range(N_HEADS):
            for c in range(n_chunks):
                heavy.append(within(h, c))
                light[len(heavy)] = [carry(h, c)]
            if h % 2 == 1:
                lo = D_POOL + (h - 1) * HEAD
                light[len(heavy)].append(out(lo, lo + 2 * HEAD, h == N_HEADS - 1))
        return heavy, light

    def emit(filler, heavy=(), light=None):
        light = light or {}
        n = max(len(heavy), 1)
        done = 0
        for i in range(len(heavy) + 1):
            for step in light.get(i, ()):
                step()
            want = len(filler) if i >= len(heavy) else (len(filler) * (i + 1)) // n
            while done < want:
                filler[done]()
                done += 1
            if i < len(heavy):
                heavy[i]()

    def store_meta(y):
        xmo_ref[0] = y

    @pl.when((b == 0) & (t == 0))
    def _():
        u_ref[0:HIST, :] = jnp.zeros((HIST, D_POOL), F32)
        s0_ref[...] = jnp.zeros(s0_ref.shape, F32)
        xm = xm_ref[0]
        emit(project(xm, 0, N_META, 0, 0))
        pool0_ref[...] = u_ref[0:HIST, :]
        emit([], *attend(xm, 0, N_META, N_META, s0_ref, store_meta))
        emit(project(x_ref[0, 0:sub, :], 0, sub, N_META, 0))

    @pl.when(t == 0)
    def _():
        st_ref[...] = s0_ref[...]

    def trip(m, carry):
        for kk in range(unroll):
            i = m * unroll + kk
            r0 = pl.multiple_of(i * sub, sub)
            x = x_ref[0, pl.ds(r0, sub), :]
            poolo_ref[0] = u_ref[HIST - POOL_BUF:HIST, :]
            if kk == unroll - 1:
                tile_end = i == n_sub - 1
                seq_end = tile_end & (t == n_tiles - 1)
                u_ref[0:HIST, :] = jnp.where(seq_end, pool0_ref[...], u_ref[0:HIST, :])
                rn = pl.multiple_of(jnp.minimum(i + 1, n_sub - 1) * sub, sub)
                x_next = jnp.where(tile_end, xn_ref[0], x_ref[0, pl.ds(rn, sub), :])
            else:
                x_next = x_ref[0, pl.ds(r0 + sub, sub), :]

            def store(y, r0=r0):
                xo_ref[0, pl.ds(r0, sub), :] = y

            emit(project(x_next, (kk + 1) % 2, sub, N_META, 0), *attend(x, kk % 2, sub, chunk, st_ref, store))
        return carry

    lax.fori_loop(0, n_sub // unroll, trip, 0)

    @pl.when(t == n_tiles - 1)
    def _():
        for h in range(N_HEADS):
            so_ref[0, h] = st_ref[h].T


def _prompt_mixer(x, xm, p, layer, acc, *, tile, sub, chunk, name):
    n_seq, t_len, _ = x.shape
    n_tiles, n_sub = t_len // tile, tile // sub
    assert t_len % tile == 0 and n_sub % 2 == 0 and sub % chunk == 0
    alias_specs, alias_args, n_alias = _stacked(acc)
    n_in = 10

    def next_rows(b, t):
        nxt = jnp.minimum(b * n_tiles + t + 1, n_seq * n_tiles - 1)
        return nxt // n_tiles, (nxt % n_tiles) * n_sub, 0

    kern = functools.partial(_prompt_mixer_kernel, n_alias=n_alias, layer=layer, tile=tile, sub=sub, chunk=chunk)
    return pl.pallas_call(
        kern,
        grid=(n_seq, n_tiles),
        in_specs=[
            pl.BlockSpec((1, tile, D_MODEL), lambda b, t: (b, t, 0)),
            pl.BlockSpec((1, sub, D_MODEL), next_rows),
            _const_spec((1, N_META, D_MODEL)),
            _layer_spec((1, D_MODEL), layer),
            _layer_spec((D_MODEL, D_IN), layer),
            _layer_spec((len(POOL_WINDOWS), POOL_GROUP, POOL_GROUP), layer),
            _layer_spec((1, D_POOL), layer),
            _const_spec((DEPTH, D_REC)),
            _layer_spec((1, HEAD), layer),
            _layer_spec((D_MODEL, D_MODEL), layer),
        ] + alias_specs,
        out_specs=[
            pl.BlockSpec((1, tile, D_MODEL), lambda b, t: (b, t, 0)),
            pl.BlockSpec((1, N_META, D_MODEL), lambda b, t: (0, 0, 0)),
            pl.BlockSpec((None, 1, POOL_BUF, D_POOL), lambda b, t: (layer, b, 0, 0)),
            pl.BlockSpec((None, 1, N_HEADS, HEAD, HEAD), lambda b, t: (layer, b, 0, 0, 0)),
        ],
        out_shape=[
            jax.ShapeDtypeStruct((n_seq, t_len, D_MODEL), F32),
            jax.ShapeDtypeStruct((1, N_META, D_MODEL), F32),
            jax.ShapeDtypeStruct((DEPTH, n_seq, POOL_BUF, D_POOL), F32),
            jax.ShapeDtypeStruct((DEPTH, n_seq, N_HEADS, HEAD, HEAD), F32),
        ],
        scratch_shapes=[
            pltpu.VMEM((sub, D_IN), F32),
            pltpu.VMEM((sub, D_IN), F32),
            pltpu.VMEM((sub, D_MODEL), BF16),
            pltpu.VMEM((sub, D_MODEL), BF16),
            pltpu.VMEM((HIST + sub, D_POOL), F32),
            pltpu.VMEM((N_HEADS, HEAD, HEAD), F32),
            pltpu.VMEM((sub, D_MODEL), BF16),
            pltpu.VMEM((HIST, D_POOL), F32),
            pltpu.VMEM((N_HEADS, HEAD, HEAD), F32),
        ],
        input_output_aliases={n_in + k: 2 + k for k in range(n_alias)},
        compiler_params=pltpu.CompilerParams(
            dimension_semantics=("arbitrary", "arbitrary"), vmem_limit_bytes=VMEM_LIMIT),
        name=name,
    )(x, x, xm, p["n1"], p["w_in"], p["w_pool"], p["pool_scale"], p["lb_raw"], p["g_norm"], p["w_out"],
      *alias_args)


def _prompt_ffn_kernel(*refs, n_alias, tile, final):
    x_ref, xm_ref, n2_ref, wa_ref, wb_ref, cw_ref, cb_ref, wd_ref, fg_ref = refs[:9]
    xo_ref, xmo_ref, convo_ref, a_ref, conv0_ref = refs[9 + n_alias:]
    b, t = pl.program_id(0), pl.program_id(1)
    lo = CONV_HIST - (CONV_W - 1)

    def ffn(x, rows):
        hn = _rmsnorm(x, n2_ref[...]).astype(BF16)
        a_ref[CONV_HIST:CONV_HIST + rows, :] = _dot(hn, wa_ref[...])
        gate = _dot(hn, wb_ref[...])
        ac = cb_ref[...]
        for j in range(CONV_W):
            ac = ac + cw_ref[j:j + 1, :] * a_ref[lo + j:lo + j + rows, :]
        act = (_gelu(ac) * gate).astype(BF16)
        return x + _dot(act, wd_ref[...])

    @pl.when((b == 0) & (t == 0))
    def _():
        a_ref[lo:CONV_HIST, :] = jnp.zeros((CONV_W - 1, D_FF), F32)
        xmo_ref[0] = ffn(xm_ref[0], N_META)
        conv0_ref[...] = a_ref[lo + N_META:CONV_HIST + N_META, :]

    @pl.when(t == 0)
    def _():
        a_ref[lo:CONV_HIST, :] = conv0_ref[...]

    xo = ffn(x_ref[0], tile)
    if final:
        xo = _rmsnorm(xo, fg_ref[...])
    xo_ref[0] = xo

    @pl.when(t == pl.num_programs(1) - 1)
    def _():
        convo_ref[0] = a_ref[lo + tile:CONV_HIST + tile, :]

    a_ref[lo:CONV_HIST, :] = a_ref[lo + tile:CONV_HIST + tile, :]


def _prompt_ffn(x, xm, p, layer, acc, *, tile, final, name):
    n_seq, t_len, _ = x.shape
    alias_specs, alias_args, n_alias = _stacked(acc)
    n_in = 9
    kern = functools.partial(_prompt_ffn_kernel, n_alias=n_alias, tile=tile, final=final)
    return pl.pallas_call(
        kern,
        grid=(n_seq, t_len // tile),
        in_specs=[
            pl.BlockSpec((1, tile, D_MODEL), lambda b, t: (b, t, 0)),
            _const_spec((1, N_META, D_MODEL)),
            _layer_spec((1, D_MODEL), layer),
            _layer_spec((D_MODEL, D_FF), layer),
            _layer_spec((D_MODEL, D_FF), layer),
            _layer_spec((CONV_W, D_FF), layer),
            _layer_spec((1, D_FF), layer),
            _layer_spec((D_FF, D_MODEL), layer),
            _const_spec((1, D_MODEL)),
        ] + alias_specs,
        out_specs=[
            pl.BlockSpec((1, tile, D_MODEL), lambda b, t: (b, t, 0)),
            pl.BlockSpec((1, N_META, D_MODEL), lambda b, t: (0, 0, 0)),
            pl.BlockSpec((None, 1, CONV_W - 1, D_FF), lambda b, t: (layer, b, 0, 0)),
        ],
        out_shape=[
            jax.ShapeDtypeStruct((n_seq, t_len, D_MODEL), F32),
            jax.ShapeDtypeStruct((1, N_META, D_MODEL), F32),
            jax.ShapeDtypeStruct((DEPTH, n_seq, CONV_W - 1, D_FF), F32),
        ],
        scratch_shapes=[
            pltpu.VMEM((CONV_HIST + tile, D_FF), F32),
            pltpu.VMEM((CONV_W - 1, D_FF), F32),
        ],
        input_output_aliases={n_in + k: 2 + k for k in range(n_alias)},
        compiler_params=pltpu.CompilerParams(
            dimension_semantics=("arbitrary", "arbitrary"), vmem_limit_bytes=VMEM_LIMIT),
        name=name,
    )(x, xm, p["n2"], p["w_a"], p["w_b"], p["conv_w"], p["conv_b"], p["w_down"], p["final_g"], *alias_args)


S_ROWS = DEC_BATCH * DEC_SEQ
S_SLOT = SUBLANES * DEC_BATCH
N_SRC = DEC_SEQ + 3
SEQ_BLOCK = 8


def _sample_mixer_kernel(*refs, n_alias, layer):
    (x_ref, n1_ref, win_ref, wpool_ref, pscale_ref, lbraw_ref, gnorm_ref, wout_ref, sp_ref, s_ref) = refs[:10]
    (xo_ref, spo_ref, so_ref, z_ref, mix_ref, oin_ref, src_ref, gather_ref, scatter_ref,
     qd8_ref, kd8_ref, v8_ref, o8_ref) = refs[10 + n_alias:]
    step = pl.program_id(0)
    tok = lambda t: slice(t * DEC_BATCH, (t + 1) * DEC_BATCH)
    rec = lambda part: [_rec_col(h, part) for h in range(N_HEADS)]

    def heads(rows, part):
        return jnp.concatenate([z_ref[rows, c] for c in rec(part)], axis=1)

    @pl.when(step == 0)
    def _():
        hn = _rmsnorm(x_ref[...], n1_ref[...]).astype(BF16)
        z_ref[...] = _dot(hn, win_ref[...])

        def ext(j, lanes):
            if j >= POOL_BUF:
                return z_ref[tok(j - POOL_BUF), lanes]
            return sp_ref[:, j, lanes]

        for gi, w in enumerate(POOL_WINDOWS):
            lanes = slice(gi * POOL_GROUP, (gi + 1) * POOL_GROUP)
            ms = []
            for t in range(DEC_SEQ):
                acc = ext(POOL_BUF + t, lanes)
                u = acc
                for j in range(1, w):
                    acc = acc + ext(POOL_BUF + t - j, lanes)
                ms.append(acc * (1.0 / w) - u)
            y = _dot(jnp.concatenate(ms, axis=0).astype(BF16), wpool_ref[gi]) * pscale_ref[:, lanes]
            mix_ref[:, lanes] = y
        full = slice(0, D_POOL)
        for j in range(POOL_BUF):
            spo_ref[:, j, :] = ext(j + DEC_SEQ, full)

        lb = _lower_bound(lbraw_ref[...], layer)
        qs, ks, gs, vs = [], [], [], []
        for t in range(DEC_SEQ):
            qs.append(_silu(heads(tok(t), 0)) * Q_SCALE)
            log_f, k = _forget_gate(heads(tok(t), 1), lb)
            ks.append(k)
            gs.append(log_f if t == 0 else gs[-1] + log_f)
            vs.append(heads(tok(t), 2))
        g_last = gs[-1]
        for t in range(DEC_SEQ):
            acc = None
            for s in range(t + 1):
                prod = qs[t] * ks[s]
                if s < t:
                    prod = prod * jnp.exp(gs[t] - gs[s])
                parts = []
                for h in range(N_HEADS):
                    hl = slice(h * HEAD, (h + 1) * HEAD)
                    a = jnp.sum(prod[:, hl], axis=-1, keepdims=True)
                    parts.append(a * vs[s][:, hl])
                term = jnp.concatenate(parts, axis=1)
                acc = term if acc is None else acc + term
            oin_ref[tok(t), :] = acc

        row = lax.broadcasted_iota(jnp.int32, (S_SLOT, N_SRC * DEC_BATCH), 0)
        col = lax.broadcasted_iota(jnp.int32, (S_SLOT, N_SRC * DEC_BATCH), 1)
        hit = ((row & (SUBLANES - 1)) == (col >> 7)) & ((row >> 3) == (col & (DEC_BATCH - 1)))
        gather_ref[...] = jnp.where(hit, 1.0, 0.0).astype(BF16)
        row = lax.broadcasted_iota(jnp.int32, (S_ROWS, S_SLOT), 0)
        col = lax.broadcasted_iota(jnp.int32, (S_ROWS, S_SLOT), 1)
        hit = ((col & (SUBLANES - 1)) == (row >> 7)) & ((col >> 3) == (row & (DEC_BATCH - 1)))
        scatter_ref[...] = jnp.where(hit, 1.0, 0.0).astype(BF16)

        for t in range(DEC_SEQ):
            src_ref[tok(t), :] = (ks[t] * jnp.exp(g_last - gs[t])).astype(BF16)
        rest = jnp.exp(g_last)
        for part in range(3):
            piece = rest.astype(BF16)
            src_ref[tok(DEC_SEQ + part), :] = piece
            rest = rest - piece.astype(F32)
        kd8_ref[...] = _dot(gather_ref[...], src_ref[...])
        for t in range(DEC_SEQ):
            src_ref[tok(t), :] = (qs[t] * jnp.exp(gs[t])).astype(BF16)
        qd8_ref[...] = _dot(gather_ref[:, 0:S_ROWS], src_ref[0:S_ROWS, :])
        for t in range(DEC_SEQ):
            src_ref[tok(t), :] = vs[t].astype(BF16)
        v8_ref[...] = _dot(gather_ref[:, 0:S_ROWS], src_ref[0:S_ROWS, :])

    slot = lax.broadcasted_iota(jnp.int32, (SUBLANES, HEAD), 0)
    ones8 = jnp.where((slot >= DEC_SEQ) & (slot < N_SRC), 1.0, 0.0).astype(BF16)
    base = pl.multiple_of(step * (SEQ_BLOCK * SUBLANES), SEQ_BLOCK * SUBLANES)
    for j in range(SEQ_BLOCK):
        rows = pl.ds(base + j * SUBLANES, SUBLANES)
        for h in range(N_HEADS):
            hl = slice(h * HEAD, (h + 1) * HEAD)
            st = s_ref[j, h]
            kd = kd8_ref[rows, hl].astype(BF16)
            o8_ref[rows, hl] = _dot(qd8_ref[rows, hl].astype(BF16), st.astype(BF16))
            so_ref[j, h] = _dot_tn(kd, ones8) * st + _dot_tn(kd, v8_ref[rows, hl].astype(BF16))

    @pl.when(step == pl.num_programs(0) - 1)
    def _():
        o8 = o8_ref[...]
        hi = o8.astype(BF16)
        lo = (o8 - hi.astype(F32)).astype(BF16)
        o = oin_ref[...] + _dot(scatter_ref[...], hi) + _dot(scatter_ref[...], lo)
        for h in range(N_HEADS):
            hl = slice(h * HEAD, (h + 1) * HEAD)
            oh = o[:, hl]
            oh = oh * lax.rsqrt(jnp.mean(oh * oh, axis=-1, keepdims=True) + EPS) * gnorm_ref[...]
            mix_ref[:, D_POOL + h * HEAD:D_POOL + (h + 1) * HEAD] = oh * _silu(z_ref[:, _rec_col(h, 3)])
        xo_ref[...] = x_ref[...] + _dot(mix_ref[...].astype(BF16), wout_ref[...])


def _sample_mixer(x, p, layer, state_pool, state_rec, acc, *, name):
    alias_specs, alias_args, n_alias = _stacked(acc)
    n_in = 10
    kern = functools.partial(_sample_mixer_kernel, n_alias=n_alias, layer=layer)
    rec_block = (None, SEQ_BLOCK, N_HEADS, HEAD, HEAD)
    return pl.pallas_call(
        kern,
        grid=(DEC_BATCH // SEQ_BLOCK,),
        in_specs=[
            _const_spec((S_ROWS, D_MODEL)),
            _layer_spec((1, D_MODEL), layer),
            _layer_spec((D_MODEL, D_IN), layer),
            _layer_spec((len(POOL_WINDOWS), POOL_GROUP, POOL_GROUP), layer),
            _layer_spec((1, D_POOL), layer),
            _const_spec((DEPTH, D_REC)),
            _layer_spec((1, HEAD), layer),
            _layer_spec((D_MODEL, D_MODEL), layer),
            _layer_spec((DEC_BATCH, POOL_BUF, D_POOL), layer),
            pl.BlockSpec(rec_block, lambda s: (layer, s, 0, 0, 0)),
        ] + alias_specs,
        out_specs=[
            pl.BlockSpec((S_ROWS, D_MODEL), lambda s: (0, 0)),
            pl.BlockSpec((None, DEC_BATCH, POOL_BUF, D_POOL), lambda s: (layer, 0, 0, 0)),
            pl.BlockSpec(rec_block, lambda s: (layer, s, 0, 0, 0)),
        ],
        out_shape=[
            jax.ShapeDtypeStruct((S_ROWS, D_MODEL), F32),
            jax.ShapeDtypeStruct((DEPTH, DEC_BATCH, POOL_BUF, D_POOL), F32),
            jax.ShapeDtypeStruct((DEPTH, DEC_BATCH, N_HEADS, HEAD, HEAD), F32),
        ],
        scratch_shapes=[
            pltpu.VMEM((S_ROWS, D_IN), F32),
            pltpu.VMEM((S_ROWS, D_MODEL), F32),
            pltpu.VMEM((S_ROWS, D_REC), F32),
            pltpu.VMEM((N_SRC * DEC_BATCH, D_REC), BF16),
            pltpu.VMEM((S_SLOT, N_SRC * DEC_BATCH), BF16),
            pltpu.VMEM((S_ROWS, S_SLOT), BF16),
            pltpu.VMEM((S_SLOT, D_REC), F32),
            pltpu.VMEM((S_SLOT, D_REC), F32),
            pltpu.VMEM((S_SLOT, D_REC), F32),
            pltpu.VMEM((S_SLOT, D_REC), F32),
        ],
        input_output_aliases={n_in + k: 1 + k for k in range(n_alias)},
        compiler_params=pltpu.CompilerParams(
            dimension_semantics=("arbitrary",), vmem_limit_bytes=VMEM_LIMIT),
        name=name,
    )(x, p["n1"], p["w_in"], p["w_pool"], p["pool_scale"], p["lb_raw"], p["g_norm"], p["w_out"],
      state_pool, state_rec, *alias_args)


def _sample_ffn_kernel(*refs, n_alias, final):
    x_ref, n2_ref, wa_ref, wb_ref, cw_ref, cb_ref, wd_ref, cs_ref, fg_ref = refs[:9]
    xo_ref, cso_ref = refs[9 + n_alias:]
    tok = lambda t: slice(t * DEC_BATCH, (t + 1) * DEC_BATCH)
    x = x_ref[...]
    hn = _rmsnorm(x, n2_ref[...]).astype(BF16)
    a = _dot(hn, wa_ref[...])
    b = _dot(hn, wb_ref[...])
    ext = [cs_ref[:, j, :] for j in range(CONV_W - 1)] + [a[tok(t)] for t in range(DEC_SEQ)]
    acs = []
    for t in range(DEC_SEQ):
        ac = cb_ref[...]
        for j in range(CONV_W):
            ac = ac + cw_ref[j:j + 1, :] * ext[t + j]
        acs.append(ac)
    act = (_gelu(jnp.concatenate(acs, axis=0)) * b).astype(BF16)
    xo = x + _dot(act, wd_ref[...])
    if final:
        xo = _rmsnorm(xo, fg_ref[...])
    xo_ref[...] = xo
    for j in range(CONV_W - 1):
        cso_ref[:, j, :] = ext[DEC_SEQ + j]


def _sample_ffn(x, p, layer, state_conv, acc, *, final, name):
    alias_specs, alias_args, n_alias = _stacked(acc)
    n_in = 9
    kern = functools.partial(_sample_ffn_kernel, n_alias=n_alias, final=final)
    return pl.pallas_call(
        kern,
        grid=(1,),
        in_specs=[
            _const_spec((S_ROWS, D_MODEL)),
            _layer_spec((1, D_MODEL), layer),
            _layer_spec((D_MODEL, D_FF), layer),
            _layer_spec((D_MODEL, D_FF), layer),
            _layer_spec((CONV_W, D_FF), layer),
            _layer_spec((1, D_FF), layer),
            _layer_spec((D_FF, D_MODEL), layer),
            _layer_spec((DEC_BATCH, CONV_W - 1, D_FF), layer),
            _const_spec((1, D_MODEL)),
        ] + alias_specs,
        out_specs=[
            pl.BlockSpec((S_ROWS, D_MODEL), lambda s: (0, 0)),
            pl.BlockSpec((None, DEC_BATCH, CONV_W - 1, D_FF), lambda s: (layer, 0, 0, 0)),
        ],
        out_shape=[
            jax.ShapeDtypeStruct((S_ROWS, D_MODEL), F32),
            jax.ShapeDtypeStruct((DEPTH, DEC_BATCH, CONV_W - 1, D_FF), F32),
        ],
        input_output_aliases={n_in + k: 1 + k for k in range(n_alias)},
        compiler_params=pltpu.CompilerParams(
            dimension_semantics=("arbitrary",), vmem_limit_bytes=VMEM_LIMIT),
        name=name,
    )(x, p["n2"], p["w_a"], p["w_b"], p["conv_w"], p["conv_b"], p["w_down"], state_conv, p["final_g"],
      *alias_args)


PROMPT_TILE = 512
PROMPT_SUB = 256
PROMPT_CHUNK = 64


def kernel(x_prompt, x_sample, state_pool, state_rec, state_conv, meta_tokens, norm1_g, w_in, w_pool, pool_scale, lb_raw, rec_norm_g, w_out, norm2_g, w_a, w_b, conv_w, conv_b, w_down, final_g):
    p = dict(
        n1=norm1_g.reshape(DEPTH, 1, D_MODEL), w_in=w_in.astype(BF16),
        w_pool=w_pool.astype(BF16), pool_scale=pool_scale.reshape(DEPTH, 1, D_POOL), lb_raw=lb_raw,
        g_norm=rec_norm_g.reshape(DEPTH, 1, HEAD), w_out=w_out.astype(BF16),
        n2=norm2_g.reshape(DEPTH, 1, D_MODEL), w_a=w_a.astype(BF16), w_b=w_b.astype(BF16),
        conv_w=conv_w, conv_b=conv_b.reshape(DEPTH, 1, D_FF), w_down=w_down.astype(BF16),
        final_g=final_g.reshape(1, D_MODEL),
    )
    last = DEPTH - 1
    n_seq = x_prompt.shape[0]
    zeros = lambda *shape: jnp.zeros((DEPTH,) + shape, F32)

    xp, xm = x_prompt, meta_tokens.reshape(1, N_META, D_MODEL)
    mix_acc = [zeros(n_seq, POOL_BUF, D_POOL), zeros(n_seq, N_HEADS, HEAD, HEAD)]
    ffn_acc = [zeros(n_seq, CONV_W - 1, D_FF)]
    for l in range(DEPTH):
        xp, xm, *mix_acc = _prompt_mixer(xp, xm, p, l, mix_acc, tile=PROMPT_TILE, sub=PROMPT_SUB,
                                         chunk=PROMPT_CHUNK, name=f"prompt_mixer_{l}")
        xp, xm, *ffn_acc = _prompt_ffn(xp, xm, p, l, ffn_acc, tile=PROMPT_TILE, final=(l == last),
                                       name=f"prompt_ffn_{l}")
    pool_p, rec_p = mix_acc
    conv_p, = ffn_acc

    xs = x_sample.transpose(1, 0, 2).reshape(S_ROWS, D_MODEL)
    mix_acc = [zeros(DEC_BATCH, POOL_BUF, D_POOL), zeros(DEC_BATCH, N_HEADS, HEAD, HEAD)]
    ffn_acc = [zeros(DEC_BATCH, CONV_W - 1, D_FF)]
    for l in range(DEPTH):
        xs, *mix_acc = _sample_mixer(xs, p, l, state_pool, state_rec, mix_acc, name=f"sample_mixer_{l}")
        xs, *ffn_acc = _sample_ffn(xs, p, l, state_conv, ffn_acc, final=(l == last), name=f"sample_ffn_{l}")
    pool_s, rec_s = mix_acc
    conv_s, = ffn_acc
    y_sample = xs.reshape(DEC_SEQ, DEC_BATCH, D_MODEL).transpose(1, 0, 2)

    return xp, y_sample, pool_p, rec_p, conv_p, pool_s, rec_s, conv_s
```

```python
import functools
import itertools

import jax
import jax.numpy as jnp
from jax import lax
from jax.experimental import pallas as pl
from jax.experimental.pallas import tpu as pltpu

F32 = jnp.float32
BF16 = jnp.bfloat16

D_MODEL = 1024
DEPTH = 4
D_POOL = 512
D_REC = 512
N_HEADS = 4
HEAD = 128
N_PARTS = 4
D_IN = D_POOL + N_PARTS * D_REC
D_FF = 2816
POOL_WINDOWS = (2, 4, 8, 16)
POOL_GROUP = 128
POOL_BUF = 15
N_META = 16
DEC_BATCH = 128
DEC_SEQ = 4
CONV_W = 3
EPS = 1e-6
Q_SCALE = HEAD ** -0.5
SQRT_HALF = 0.7071067811865476

SUBLANES = 8
MXU_COLS = 256
HIST = 16
CONV_HIST = 8
VMEM_LIMIT = 56 * 1024 * 1024


def _rmsnorm(x, g):
    return x * lax.rsqrt(jnp.mean(x * x, axis=-1, keepdims=True) + EPS) * g


def _silu(x):
    return x * (1.0 / (1.0 + jnp.exp(-x)))


def _gelu(x):
    return 0.5 * x * (1.0 + lax.erf(x * SQRT_HALF))


def _lower_bound(lb_raw, layer):
    m = jnp.max(lb_raw, axis=0, keepdims=True)
    e = jnp.exp(lb_raw - m)
    p = e / jnp.sum(e, axis=0, keepdims=True)
    lb = jnp.zeros_like(m)
    for i in range(1, layer + 1):
        lb = lb + p[i:i + 1]
    return lb


def _forget_gate(f, lb):
    m = jnp.exp(-jnp.abs(f))
    r = 1.0 / (1.0 + m)
    sig_neg = jnp.where(f >= 0, m * r, r)
    log_sig = jnp.minimum(f, 0.0) - jnp.log(1.0 + m)
    a = jnp.log(lb)
    b = jnp.log(1.0 - lb) + log_sig
    log_f = jnp.maximum(a, b) + jnp.log(1.0 + jnp.exp(-jnp.abs(a - b)))
    return log_f, (1.0 - lb) * sig_neg


def _dot(a, b):
    return jnp.dot(a, b, preferred_element_type=F32)


def _dot_nt(a, b):
    return lax.dot_general(a, b, (((1,), (1,)), ((), ())), preferred_element_type=F32)


def _dot_tn(a, b):
    return lax.dot_general(a, b, (((0,), (0,)), ((), ())), preferred_element_type=F32)


HALF = SUBLANES // 2


def _gla_masks(n_blk):
    c_len = SUBLANES * n_blk
    lane = lax.broadcasted_iota(jnp.int32, (SUBLANES, c_len), 1)
    row = lax.broadcasted_iota(jnp.int32, (SUBLANES, c_len), 0)
    diag = lane - row
    near = [[diag == SUBLANES * i - r for r in range(HALF)] for i in range(n_blk)]
    across = [(row >= HALF) & (lane >= SUBLANES * i) & (lane < SUBLANES * i + HALF) for i in range(n_blk)]
    below = {c: lane < SUBLANES * c for c in range(1, n_blk)}
    return near, across, below


def _gla_intra(q, k, g, v, n_blk, masks):
    near, across, below = masks
    sub = lax.broadcasted_iota(jnp.int32, (SUBLANES, HEAD), 0)
    blk = lambda a, i: a[SUBLANES * i:SUBLANES * (i + 1)]
    square = lambda x: _dot_nt(x, x)

    local, tot = [], []
    for i in range(n_blk):
        x = blk(g, i)
        for s in (1, 2, 4):
            x = x + jnp.where(sub >= s, pltpu.roll(x, s, 0), 0.0)
        local.append(x)
        tot.append(jnp.exp(x[SUBLANES - 1:SUBLANES]))

    spans = {}

    def span(a, b):
        if a >= b:
            return None
        if (a, b) not in spans:
            head = span(a, b - 1)
            spans[a, b] = tot[b - 1] if head is None else head * tot[b - 1]
        return spans[a, b]

    scaled = lambda x, s: x if s is None else x * s
    q_in = [blk(q, i) * jnp.exp(local[i]) for i in range(n_blk)]
    k_out = [blk(k, i) * jnp.exp(local[i][SUBLANES - 1:SUBLANES] - local[i]) for i in range(n_blk)]
    rows_bf16 = lambda tiles: jnp.concatenate(tiles, axis=0).astype(BF16)
    q_dec = rows_bf16([scaled(q_in[i], span(0, i)) for i in range(n_blk)])
    k_dec = rows_bf16([scaled(k_out[i], span(i + 1, n_blk)) for i in range(n_blk)])

    level_prod = {}
    half = n_blk // 2
    while half >= 1:
        tiles = []
        for i in range(n_blk):
            cut = (i // (2 * half)) * 2 * half + half
            tiles.append(scaled(q_in[i], span(cut, i)) if i >= cut else scaled(k_out[i], span(i + 1, cut)))
        level_prod[half] = square(rows_bf16(tiles))
        half //= 2

    tiles = []
    for i in range(n_blk):
        mid = jnp.broadcast_to(local[i][HALF - 1:HALF], (SUBLANES, HEAD))
        tiles.append(jnp.where(sub >= HALF, blk(q, i), blk(k, i)) * jnp.exp(-jnp.abs(local[i] - mid)))
    half_prod = square(rows_bf16(tiles))

    rows = []
    for i in range(n_blk):
        ki = blk(k, i)
        gate = jnp.exp(blk(g, i))
        row = jnp.where(across[i], blk(half_prod, i), 0.0)
        w = blk(q, i)
        for r in range(HALF):
            if r > 0:
                w = w * (gate if r == 1 else pltpu.roll(gate, r - 1, 0))
            a = jnp.sum(w * (ki if r == 0 else pltpu.roll(ki, r, 0)), axis=-1, keepdims=True)
            row = jnp.where(near[i][r], a, row)
        half = 1
        while half < n_blk:
            if i % (2 * half) >= half:
                row = jnp.where(below[(i // (2 * half)) * 2 * half + half], blk(level_prod[half], i), row)
            half *= 2
        rows.append(row)

    v_b = v.astype(BF16)
    return _dot(rows_bf16(rows), v_b), q_dec, k_dec, v_b, span(0, n_blk)


def _gla_state(intra, st):
    o_intra, q_dec, k_dec, v_b, decay = intra
    out = o_intra + _dot_nt(q_dec, st.astype(BF16))
    return out, st * decay + _dot_tn(v_b, k_dec)


def _pool_count(start, offset, rows, w):
    if start >= max(POOL_WINDOWS) - 1:
        return 1.0 / w
    pos = start + offset + lax.broadcasted_iota(jnp.int32, (rows, 1), 0)
    return 1.0 / jnp.minimum(pos + 1, w).astype(F32)


def _rec_col(h, part):
    lo = D_POOL + part * D_REC + h * HEAD
    return slice(lo, lo + HEAD)


def _layer_spec(shape, layer):
    zeros = (0,) * len(shape)
    return pl.BlockSpec((None,) + tuple(shape), lambda *_: (layer,) + zeros, pipeline_mode=pl.Buffered(1))


def _const_spec(shape):
    zeros = (0,) * len(shape)
    return pl.BlockSpec(tuple(shape), lambda *_: zeros, pipeline_mode=pl.Buffered(1))


def _stacked(acc):
    return [pl.BlockSpec(memory_space=pl.ANY)] * len(acc), list(acc), len(acc)


def _prompt_mixer_kernel(*refs, n_alias, layer, tile, sub, chunk):
    (x_ref, xn_ref, xm_ref, n1_ref, win_ref, wpool_ref, pscale_ref, lbraw_ref, gnorm_ref, wout_ref) = refs[:10]
    (xo_ref, xmo_ref, poolo_ref, so_ref, z0_ref, z1_ref, mix0_ref, mix1_ref, u_ref, st_ref, hn_ref,
     pool0_ref, s0_ref, acc_ref) = refs[10 + n_alias:]
    z_refs, mix_refs = (z0_ref, z1_ref), (mix0_ref, mix1_ref)
    b, t, n_tiles = pl.program_id(0), pl.program_id(1), pl.num_programs(1)
    n_sub = tile // sub
    unroll = 2
    lb = _lower_bound(lbraw_ref[...], layer)
    gnorm = gnorm_ref[...]

    def project(x, slot, rows, start, offset):
        z_ref, mix_ref = z_refs[slot], mix_refs[slot]

        def norm():
            hn_ref[0:rows, :] = _rmsnorm(x(), n1_ref[...]).astype(BF16)

        def columns(n):
            cols = slice(n * MXU_COLS, (n + 1) * MXU_COLS)

            def step():
                z_ref[0:rows, cols] = _dot(hn_ref[0:rows, :], win_ref[:, cols])
            return step

        def pool():
            u_ref[HIST:HIST + rows, :] = z_ref[0:rows, 0:D_POOL]
            for gi, w in enumerate(POOL_WINDOWS):
                lanes = slice(gi * POOL_GROUP, (gi + 1) * POOL_GROUP)
                acc = u_ref[HIST:HIST + rows, lanes]
                u = acc
                for j in range(1, w):
                    acc = acc + u_ref[HIST - j:HIST - j + rows, lanes]
                m = acc * _pool_count(start, offset, rows, w) - u
                y = _dot(m.astype(BF16), wpool_ref[gi]) * pscale_ref[:, lanes]
                mix_ref[0:rows, lanes] = y.astype(BF16)
            u_ref[0:HIST, :] = u_ref[rows:rows + HIST, :]

        n_pool = D_POOL // MXU_COLS
        tiles = [columns(n) for n in range(D_IN // MXU_COLS)]
        return [norm] + tiles[:n_pool] + [pool] + tiles[n_pool:]

    def attend(x, slot, rows, chunk, state_ref, store):
        z_ref, mix_ref = z_refs[slot], mix_refs[slot]
        n_blk, n_chunks = chunk // SUBLANES, rows // chunk
        intra = {}
        masks = _gla_masks(n_blk)

        def within(h, c):
            rows = slice(c * chunk, (c + 1) * chunk)

            def step():
                q = _silu(z_ref[rows, _rec_col(h, 0)]) * Q_SCALE
                log_f, k = _forget_gate(z_ref[rows, _rec_col(h, 1)], lb[:, h * HEAD:(h + 1) * HEAD])
                intra[h, c] = _gla_intra(q, k, log_f, z_ref[rows, _rec_col(h, 2)], n_blk, masks)
            return step

        state = {}

        def carry(h, c):
            rows = slice(c * chunk, (c + 1) * chunk)

            def step():
                st = state_ref[h] if c == 0 else state[h]
                o, state[h] = _gla_state(intra.pop((h, c)), st)
                o = o * lax.rsqrt(jnp.mean(o * o, axis=-1, keepdims=True) + EPS) * gnorm
                o = o * _silu(z_ref[rows, _rec_col(h, 3)])
                mix_ref[rows, D_POOL + h * HEAD:D_POOL + (h + 1) * HEAD] = o.astype(BF16)
                if c == n_chunks - 1:
                    state_ref[h] = state.pop(h)
            return step

        def out(lo, hi, first, last):
            def step():
                part = _dot(mix_ref[0:rows, lo:hi], wout_ref[lo:hi, :])
                if first:
                    acc_ref[0:rows, :] = part
                elif last:
                    store(x() + acc_ref[0:rows, :] + part)
                else:
                    acc_ref[0:rows, :] += part
            return step

        heavy, light = [], {}
        light[0] = [out(0, D_POOL, True, False)]
        for h in range(N_HEADS):
            for c in range(n_chunks):
                heavy.append(within(h, c))
                light[len(heavy)] = [carry(h, c)]
            if h % 2 == 1:
                lo = D_POOL + (h - 1) * HEAD
                light[len(heavy)].append(out(lo, lo + 2 * HEAD, False, h == N_HEADS - 1))
        return heavy, light

    def emit(filler, heavy=(), light=None):
        light = light or {}
        n = max(len(heavy), 1)
        done = 0
        for i in range(len(heavy) + 1):
            for step in light.get(i, ()):
                step()
            want = len(filler) if i >= len(heavy) else (len(filler) * (i + 1)) // n
            while done < want:
                filler[done]()
                done += 1
            if i < len(heavy):
                heavy[i]()

    def store_meta(y):
        xmo_ref[0] = y

    @pl.when((b == 0) & (t == 0))
    def _():
        u_ref[0:HIST, :] = jnp.zeros((HIST, D_POOL), F32)
        s0_ref[...] = jnp.zeros(s0_ref.shape, F32)
        xm = lambda: xm_ref[0]
        emit(project(xm, 0, N_META, 0, 0))
        pool0_ref[...] = u_ref[0:HIST, :]
        emit([], *attend(xm, 0, N_META, N_META, s0_ref, store_meta))
        emit(project(lambda: x_ref[0, 0:sub, :], 0, sub, N_META, 0))

    @pl.when(t == 0)
    def _():
        st_ref[...] = s0_ref[...]

    def trip(m, carry):
        for kk in range(unroll):
            i = m * unroll + kk
            r0 = pl.multiple_of(i * sub, sub)
            poolo_ref[0] = u_ref[HIST - POOL_BUF:HIST, :]
            if kk == unroll - 1:
                tile_end = i == n_sub - 1
                seq_end = tile_end & (t == n_tiles - 1)
                u_ref[0:HIST, :] = jnp.where(seq_end, pool0_ref[...], u_ref[0:HIST, :])
                rn = pl.multiple_of(jnp.minimum(i + 1, n_sub - 1) * sub, sub)

                def x_next(tile_end=tile_end, rn=rn):
                    return jnp.where(tile_end, xn_ref[0], x_ref[0, pl.ds(rn, sub), :])
            else:
                def x_next(r0=r0):
                    return x_ref[0, pl.ds(r0 + sub, sub), :]

            def x(r0=r0):
                return x_ref[0, pl.ds(r0, sub), :]

            def store(y, r0=r0):
                xo_ref[0, pl.ds(r0, sub), :] = y

            emit(project(x_next, (kk + 1) % 2, sub, N_META, 0), *attend(x, kk % 2, sub, chunk, st_ref, store))
        return carry

    lax.fori_loop(0, n_sub // unroll, trip, 0)

    @pl.when(t == n_tiles - 1)
    def _():
        for h in range(N_HEADS):
            so_ref[0, h] = st_ref[h].T


def _prompt_mixer(x, xm, p, layer, acc, *, tile, sub, chunk, name):
    n_seq, t_len, _ = x.shape
    n_tiles, n_sub = t_len // tile, tile // sub
    assert t_len % tile == 0 and n_sub % 2 == 0 and sub % chunk == 0
    alias_specs, alias_args, n_alias = _stacked(acc)
    n_in = 10

    def next_rows(b, t):
        nxt = jnp.minimum(b * n_tiles + t + 1, n_seq * n_tiles - 1)
        return nxt // n_tiles, (nxt % n_tiles) * n_sub, 0

    kern = functools.partial(_prompt_mixer_kernel, n_alias=n_alias, layer=layer, tile=tile, sub=sub, chunk=chunk)
    return pl.pallas_call(
        kern,
        grid=(n_seq, n_tiles),
        in_specs=[
            pl.BlockSpec((1, tile, D_MODEL), lambda b, t: (b, t, 0)),
            pl.BlockSpec((1, sub, D_MODEL), next_rows),
            _const_spec((1, N_META, D_MODEL)),
            _layer_spec((1, D_MODEL), layer),
            _layer_spec((D_MODEL, D_IN), layer),
            _layer_spec((len(POOL_WINDOWS), POOL_GROUP, POOL_GROUP), layer),
            _layer_spec((1, D_POOL), layer),
            _const_spec((DEPTH, D_REC)),
            _layer_spec((1, HEAD), layer),
            _layer_spec((D_MODEL, D_MODEL), layer),
        ] + alias_specs,
        out_specs=[
            pl.BlockSpec((1, tile, D_MODEL), lambda b, t: (b, t, 0)),
            pl.BlockSpec((1, N_META, D_MODEL), lambda b, t: (0, 0, 0)),
            pl.BlockSpec((None, 1, POOL_BUF, D_POOL), lambda b, t: (layer, b, 0, 0)),
            pl.BlockSpec((None, 1, N_HEADS, HEAD, HEAD), lambda b, t: (layer, b, 0, 0, 0)),
        ],
        out_shape=[
            jax.ShapeDtypeStruct((n_seq, t_len, D_MODEL), F32),
            jax.ShapeDtypeStruct((1, N_META, D_MODEL), F32),
            jax.ShapeDtypeStruct((DEPTH, n_seq, POOL_BUF, D_POOL), F32),
            jax.ShapeDtypeStruct((DEPTH, n_seq, N_HEADS, HEAD, HEAD), F32),
        ],
        scratch_shapes=[
            pltpu.VMEM((sub, D_IN), F32),
            pltpu.VMEM((sub, D_IN), F32),
            pltpu.VMEM((sub, D_MODEL), BF16),
            pltpu.VMEM((sub, D_MODEL), BF16),
            pltpu.VMEM((HIST + sub, D_POOL), F32),
            pltpu.VMEM((N_HEADS, HEAD, HEAD), F32),
            pltpu.VMEM((sub, D_MODEL), BF16),
            pltpu.VMEM((HIST, D_POOL), F32),
            pltpu.VMEM((N_HEADS, HEAD, HEAD), F32),
            pltpu.VMEM((sub, D_MODEL), F32),
        ],
        input_output_aliases={n_in + k: 2 + k for k in range(n_alias)},
        compiler_params=pltpu.CompilerParams(
            dimension_semantics=("arbitrary", "arbitrary"), vmem_limit_bytes=VMEM_LIMIT),
        name=name,
    )(x, x, xm, p["n1"], p["w_in"], p["w_pool"], p["pool_scale"], p["lb_raw"], p["g_norm"], p["w_out"],
      *alias_args)


def _prompt_ffn_kernel(*refs, n_alias, tile, final):
    x_ref, xm_ref, n2_ref, wa_ref, wb_ref, cw_ref, cb_ref, wd_ref, fg_ref = refs[:9]
    xo_ref, xmo_ref, convo_ref, a_ref, conv0_ref = refs[9 + n_alias:]
    b, t = pl.program_id(0), pl.program_id(1)
    lo = CONV_HIST - (CONV_W - 1)

    def ffn(x, rows):
        hn = _rmsnorm(x, n2_ref[...]).astype(BF16)
        a_ref[CONV_HIST:CONV_HIST + rows, :] = _dot(hn, wa_ref[...])
        gate = _dot(hn, wb_ref[...])
        ac = cb_ref[...]
        for j in range(CONV_W):
            ac = ac + cw_ref[j:j + 1, :] * a_ref[lo + j:lo + j + rows, :]
        act = (_gelu(ac) * gate).astype(BF16)
        return x + _dot(act, wd_ref[...])

    @pl.when((b == 0) & (t == 0))
    def _():
        a_ref[lo:CONV_HIST, :] = jnp.zeros((CONV_W - 1, D_FF), F32)
        xmo_ref[0] = ffn(xm_ref[0], N_META)
        conv0_ref[...] = a_ref[lo + N_META:CONV_HIST + N_META, :]

    @pl.when(t == 0)
    def _():
        a_ref[lo:CONV_HIST, :] = conv0_ref[...]

    xo = ffn(x_ref[0], tile)
    if final:
        xo = _rmsnorm(xo, fg_ref[...])
    xo_ref[0] = xo

    @pl.when(t == pl.num_programs(1) - 1)
    def _():
        convo_ref[0] = a_ref[lo + tile:CONV_HIST + tile, :]

    a_ref[lo:CONV_HIST, :] = a_ref[lo + tile:CONV_HIST + tile, :]


def _prompt_ffn(x, xm, p, layer, weights, acc, *, tile, final, name):
    n_seq, t_len, _ = x.shape
    w_a, w_b, w_down = weights
    alias_specs, alias_args, n_alias = _stacked(acc)
    n_in = 9
    kern = functools.partial(_prompt_ffn_kernel, n_alias=n_alias, tile=tile, final=final)
    return pl.pallas_call(
        kern,
        grid=(n_seq, t_len // tile),
        in_specs=[
            pl.BlockSpec((1, tile, D_MODEL), lambda b, t: (b, t, 0)),
            _const_spec((1, N_META, D_MODEL)),
            _layer_spec((1, D_MODEL), layer),
            _const_spec((D_MODEL, D_FF)),
            _const_spec((D_MODEL, D_FF)),
            _layer_spec((CONV_W, D_FF), layer),
            _layer_spec((1, D_FF), layer),
            _const_spec((D_FF, D_MODEL)),
            _const_spec((1, D_MODEL)),
        ] + alias_specs,
        out_specs=[
            pl.BlockSpec((1, tile, D_MODEL), lambda b, t: (b, t, 0)),
            pl.BlockSpec((1, N_META, D_MODEL), lambda b, t: (0, 0, 0)),
            pl.BlockSpec((None, 1, CONV_W - 1, D_FF), lambda b, t: (layer, b, 0, 0)),
        ],
        out_shape=[
            jax.ShapeDtypeStruct((n_seq, t_len, D_MODEL), F32),
            jax.ShapeDtypeStruct((1, N_META, D_MODEL), F32),
            jax.ShapeDtypeStruct((DEPTH, n_seq, CONV_W - 1, D_FF), F32),
        ],
        scratch_shapes=[
            pltpu.VMEM((CONV_HIST + tile, D_FF), F32),
            pltpu.VMEM((CONV_W - 1, D_FF), F32),
        ],
        input_output_aliases={n_in + k: 2 + k for k in range(n_alias)},
        compiler_params=pltpu.CompilerParams(
            dimension_semantics=("arbitrary", "arbitrary"), vmem_limit_bytes=VMEM_LIMIT),
        name=name,
    )(x, xm, p["n2"], w_a, w_b, p["conv_w"], p["conv_b"], w_down, p["final_g"], *alias_args)


S_ROWS = DEC_BATCH * DEC_SEQ
S_SLOT = SUBLANES * DEC_BATCH
N_SRC = DEC_SEQ + 3
SEQ_BLOCK = 8


def _sample_mixer_kernel(*refs, n_alias, layer):
    (x_ref, n1_ref, win_ref, wpool_ref, pscale_ref, lbraw_ref, gnorm_ref, wout_ref, sp_ref, s_ref) = refs[:10]
    (xo_ref, spo_ref, so_ref, z_ref, mix_ref, oin_ref, src_ref, gather_ref, scatter_ref,
     qd8_ref, kd8_ref, v8_ref, o8_ref) = refs[10 + n_alias:]
    step = pl.program_id(0)
    tok = lambda t: slice(t * DEC_BATCH, (t + 1) * DEC_BATCH)
    rec = lambda part: [_rec_col(h, part) for h in range(N_HEADS)]

    def heads(rows, part):
        return jnp.concatenate([z_ref[rows, c] for c in rec(part)], axis=1)

    @pl.when(step == 0)
    def _():
        hn = _rmsnorm(x_ref[...], n1_ref[...]).astype(BF16)
        z_ref[...] = _dot(hn, win_ref[...])

        def ext(j, lanes):
            if j >= POOL_BUF:
                return z_ref[tok(j - POOL_BUF), lanes]
            return sp_ref[j, :, lanes]

        for gi, w in enumerate(POOL_WINDOWS):
            lanes = slice(gi * POOL_GROUP, (gi + 1) * POOL_GROUP)
            ms = []
            for t in range(DEC_SEQ):
                acc = ext(POOL_BUF + t, lanes)
                u = acc
                for j in range(1, w):
                    acc = acc + ext(POOL_BUF + t - j, lanes)
                ms.append(acc * (1.0 / w) - u)
            y = _dot(jnp.concatenate(ms, axis=0).astype(BF16), wpool_ref[gi]) * pscale_ref[:, lanes]
            mix_ref[:, lanes] = y
        full = slice(0, D_POOL)
        for j in range(POOL_BUF):
            spo_ref[j] = ext(j + DEC_SEQ, full)

        lb = _lower_bound(lbraw_ref[...], layer)
        qs, ks, gs, vs = [], [], [], []
        for t in range(DEC_SEQ):
            qs.append(_silu(heads(tok(t), 0)) * Q_SCALE)
            log_f, k = _forget_gate(heads(tok(t), 1), lb)
            ks.append(k)
            gs.append(log_f if t == 0 else gs[-1] + log_f)
            vs.append(heads(tok(t), 2))
        g_last = gs[-1]
        for t in range(DEC_SEQ):
            acc = None
            for s in range(t + 1):
                prod = qs[t] * ks[s]
                if s < t:
                    prod = prod * jnp.exp(gs[t] - gs[s])
                parts = []
                for h in range(N_HEADS):
                    hl = slice(h * HEAD, (h + 1) * HEAD)
                    a = jnp.sum(prod[:, hl], axis=-1, keepdims=True)
                    parts.append(a * vs[s][:, hl])
                term = jnp.concatenate(parts, axis=1)
                acc = term if acc is None else acc + term
            oin_ref[tok(t), :] = acc

        row = lax.broadcasted_iota(jnp.int32, (S_SLOT, N_SRC * DEC_BATCH), 0)
        col = lax.broadcasted_iota(jnp.int32, (S_SLOT, N_SRC * DEC_BATCH), 1)
        hit = ((row & (SUBLANES - 1)) == (col >> 7)) & ((row >> 3) == (col & (DEC_BATCH - 1)))
        gather_ref[...] = jnp.where(hit, 1.0, 0.0).astype(BF16)
        row = lax.broadcasted_iota(jnp.int32, (S_ROWS, S_SLOT), 0)
        col = lax.broadcasted_iota(jnp.int32, (S_ROWS, S_SLOT), 1)
        hit = ((col & (SUBLANES - 1)) == (row >> 7)) & ((col >> 3) == (row & (DEC_BATCH - 1)))
        scatter_ref[...] = jnp.where(hit, 1.0, 0.0).astype(BF16)

        for t in range(DEC_SEQ):
            src_ref[tok(t), :] = (ks[t] * jnp.exp(g_last - gs[t])).astype(BF16)
        rest = jnp.exp(g_last)
        for part in range(3):
            piece = rest.astype(BF16)
            src_ref[tok(DEC_SEQ + part), :] = piece
            rest = rest - piece.astype(F32)
        kd8_ref[...] = _dot(gather_ref[...], src_ref[...])
        for t in range(DEC_SEQ):
            src_ref[tok(t), :] = (qs[t] * jnp.exp(gs[t])).astype(BF16)
        qd8_ref[...] = _dot(gather_ref[:, 0:S_ROWS], src_ref[0:S_ROWS, :])
        for t in range(DEC_SEQ):
            src_ref[tok(t), :] = vs[t].astype(BF16)
        v8_ref[...] = _dot(gather_ref[:, 0:S_ROWS], src_ref[0:S_ROWS, :])

    slot = lax.broadcasted_iota(jnp.int32, (SUBLANES, HEAD), 0)
    ones8 = jnp.where((slot >= DEC_SEQ) & (slot < N_SRC), 1.0, 0.0).astype(BF16)
    base = pl.multiple_of(step * (SEQ_BLOCK * SUBLANES), SEQ_BLOCK * SUBLANES)
    for j in range(SEQ_BLOCK):
        rows = pl.ds(base + j * SUBLANES, SUBLANES)
        for h in range(N_HEADS):
            hl = slice(h * HEAD, (h + 1) * HEAD)
            st = s_ref[j, h]
            kd = kd8_ref[rows, hl].astype(BF16)
            o8_ref[rows, hl] = _dot(qd8_ref[rows, hl].astype(BF16), st.astype(BF16))
            so_ref[j, h] = _dot_tn(kd, ones8) * st + _dot_tn(kd, v8_ref[rows, hl].astype(BF16))

    @pl.when(step == pl.num_programs(0) - 1)
    def _():
        o8 = o8_ref[...]
        hi = o8.astype(BF16)
        lo = (o8 - hi.astype(F32)).astype(BF16)
        o = oin_ref[...] + _dot(scatter_ref[...], hi) + _dot(scatter_ref[...], lo)
        for h in range(N_HEADS):
            hl = slice(h * HEAD, (h + 1) * HEAD)
            oh = o[:, hl]
            oh = oh * lax.rsqrt(jnp.mean(oh * oh, axis=-1, keepdims=True) + EPS) * gnorm_ref[...]
            mix_ref[:, D_POOL + h * HEAD:D_POOL + (h + 1) * HEAD] = oh * _silu(z_ref[:, _rec_col(h, 3)])
        xo_ref[...] = x_ref[...] + _dot(mix_ref[...].astype(BF16), wout_ref[...])


def _sample_mixer(x, p, layer, state_pool, state_rec, acc, *, name):
    alias_specs, alias_args, n_alias = _stacked(acc)
    n_in = 10
    kern = functools.partial(_sample_mixer_kernel, n_alias=n_alias, layer=layer)
    rec_block = (None, SEQ_BLOCK, N_HEADS, HEAD, HEAD)
    return pl.pallas_call(
        kern,
        grid=(DEC_BATCH // SEQ_BLOCK,),
        in_specs=[
            _const_spec((S_ROWS, D_MODEL)),
            _layer_spec((1, D_MODEL), layer),
            _layer_spec((D_MODEL, D_IN), layer),
            _layer_spec((len(POOL_WINDOWS), POOL_GROUP, POOL_GROUP), layer),
            _layer_spec((1, D_POOL), layer),
            _const_spec((DEPTH, D_REC)),
            _layer_spec((1, HEAD), layer),
            _layer_spec((D_MODEL, D_MODEL), layer),
            _layer_spec((POOL_BUF, DEC_BATCH, D_POOL), layer),
            pl.BlockSpec(rec_block, lambda s: (layer, s, 0, 0, 0)),
        ] + alias_specs,
        out_specs=[
            pl.BlockSpec((S_ROWS, D_MODEL), lambda s: (0, 0)),
            pl.BlockSpec((None, POOL_BUF, DEC_BATCH, D_POOL), lambda s: (layer, 0, 0, 0)),
            pl.BlockSpec(rec_block, lambda s: (layer, s, 0, 0, 0)),
        ],
        out_shape=[
            jax.ShapeDtypeStruct((S_ROWS, D_MODEL), F32),
            jax.ShapeDtypeStruct((DEPTH, POOL_BUF, DEC_BATCH, D_POOL), F32),
            jax.ShapeDtypeStruct((DEPTH, DEC_BATCH, N_HEADS, HEAD, HEAD), F32),
        ],
        scratch_shapes=[
            pltpu.VMEM((S_ROWS, D_IN), F32),
            pltpu.VMEM((S_ROWS, D_MODEL), F32),
            pltpu.VMEM((S_ROWS, D_REC), F32),
            pltpu.VMEM((N_SRC * DEC_BATCH, D_REC), BF16),
            pltpu.VMEM((S_SLOT, N_SRC * DEC_BATCH), BF16),
            pltpu.VMEM((S_ROWS, S_SLOT), BF16),
            pltpu.VMEM((S_SLOT, D_REC), F32),
            pltpu.VMEM((S_SLOT, D_REC), F32),
            pltpu.VMEM((S_SLOT, D_REC), F32),
            pltpu.VMEM((S_SLOT, D_REC), F32),
        ],
        input_output_aliases={n_in + k: 1 + k for k in range(n_alias)},
        compiler_params=pltpu.CompilerParams(
            dimension_semantics=("arbitrary",), vmem_limit_bytes=VMEM_LIMIT),
        name=name,
    )(x, p["n1"], p["w_in"], p["w_pool"], p["pool_scale"], p["lb_raw"], p["g_norm"], p["w_out"],
      state_pool, state_rec, *alias_args)


FFN_BLOCK = MXU_COLS


def _sample_ffn_kernel(*refs, n_alias, final):
    x_ref, n2_ref, wa_ref, wb_ref, cw_ref, cb_ref, wd_ref, cs_ref, fg_ref = refs[:9]
    xo_ref, cso_ref, wao_ref, wbo_ref, wdo_ref, hn_ref, acc_ref = refs[9 + n_alias:]
    step = pl.program_id(0)
    tok = lambda t: slice(t * DEC_BATCH, (t + 1) * DEC_BATCH)

    @pl.when(step == 0)
    def _():
        hn_ref[...] = _rmsnorm(x_ref[...], n2_ref[...]).astype(BF16)
        acc_ref[...] = jnp.zeros(acc_ref.shape, F32)

    wa, wb, wd = wa_ref[...].astype(BF16), wb_ref[...].astype(BF16), wd_ref[...].astype(BF16)
    wao_ref[...] = wa
    wbo_ref[...] = wb
    wdo_ref[...] = wd
    hn = hn_ref[...]
    a = _dot(hn, wa)
    gate = _dot(hn, wb)
    ext = [cs_ref[:, j, :] for j in range(CONV_W - 1)] + [a[tok(t)] for t in range(DEC_SEQ)]
    acs = []
    for t in range(DEC_SEQ):
        ac = cb_ref[...]
        for j in range(CONV_W):
            ac = ac + cw_ref[j:j + 1, :] * ext[t + j]
        acs.append(ac)
    act = (_gelu(jnp.concatenate(acs, axis=0)) * gate).astype(BF16)
    acc_ref[...] += _dot(act, wd)
    for j in range(CONV_W - 1):
        cso_ref[:, j, :] = ext[DEC_SEQ + j]

    @pl.when(step == pl.num_programs(0) - 1)
    def _():
        xo = x_ref[...] + acc_ref[...]
        if final:
            xo = _rmsnorm(xo, fg_ref[...])
        xo_ref[...] = xo


def _sample_ffn(x, p, layer, state_conv, acc, *, final, name):
    alias_specs, alias_args, n_alias = _stacked(acc)
    n_in = 9
    kern = functools.partial(_sample_ffn_kernel, n_alias=n_alias, final=final)
    cols = lambda shape: pl.BlockSpec((None,) + shape, lambda j: (layer,) + (0,) * (len(shape) - 1) + (j,))
    return pl.pallas_call(
        kern,
        grid=(D_FF // FFN_BLOCK,),
        in_specs=[
            _const_spec((S_ROWS, D_MODEL)),
            _layer_spec((1, D_MODEL), layer),
            cols((D_MODEL, FFN_BLOCK)),
            cols((D_MODEL, FFN_BLOCK)),
            cols((CONV_W, FFN_BLOCK)),
            cols((1, FFN_BLOCK)),
            pl.BlockSpec((None, FFN_BLOCK, D_MODEL), lambda j: (layer, j, 0)),
            cols((DEC_BATCH, CONV_W - 1, FFN_BLOCK)),
            _const_spec((1, D_MODEL)),
        ] + alias_specs,
        out_specs=[
            pl.BlockSpec((S_ROWS, D_MODEL), lambda j: (0, 0)),
            cols((DEC_BATCH, CONV_W - 1, FFN_BLOCK)),
            pl.BlockSpec((D_MODEL, FFN_BLOCK), lambda j: (0, j)),
            pl.BlockSpec((D_MODEL, FFN_BLOCK), lambda j: (0, j)),
            pl.BlockSpec((FFN_BLOCK, D_MODEL), lambda j: (j, 0)),
        ],
        out_shape=[
            jax.ShapeDtypeStruct((S_ROWS, D_MODEL), F32),
            jax.ShapeDtypeStruct((DEPTH, DEC_BATCH, CONV_W - 1, D_FF), F32),
            jax.ShapeDtypeStruct((D_MODEL, D_FF), BF16),
            jax.ShapeDtypeStruct((D_MODEL, D_FF), BF16),
            jax.ShapeDtypeStruct((D_FF, D_MODEL), BF16),
        ],
        scratch_shapes=[
            pltpu.VMEM((S_ROWS, D_MODEL), BF16),
            pltpu.VMEM((S_ROWS, D_MODEL), F32),
        ],
        input_output_aliases={n_in + k: 1 + k for k in range(n_alias)},
        compiler_params=pltpu.CompilerParams(
            dimension_semantics=("arbitrary",), vmem_limit_bytes=VMEM_LIMIT),
        name=name,
    )(x, p["n2"], p["w_a"], p["w_b"], p["conv_w"], p["conv_b"], p["w_down"], state_conv, p["final_g"],
      *alias_args)


PROMPT_TILE = 512
PROMPT_SUB = 256
PROMPT_CHUNK = 64


def kernel(x_prompt, x_sample, state_pool, state_rec, state_conv, meta_tokens, norm1_g, w_in, w_pool, pool_scale, lb_raw, rec_norm_g, w_out, norm2_g, w_a, w_b, conv_w, conv_b, w_down, final_g):
    p = dict(
        n1=norm1_g.reshape(DEPTH, 1, D_MODEL), w_in=w_in.astype(BF16),
        w_pool=w_pool.astype(BF16), pool_scale=pool_scale.reshape(DEPTH, 1, D_POOL), lb_raw=lb_raw,
        g_norm=rec_norm_g.reshape(DEPTH, 1, HEAD), w_out=w_out.astype(BF16),
        n2=norm2_g.reshape(DEPTH, 1, D_MODEL), w_a=w_a, w_b=w_b,
        conv_w=conv_w, conv_b=conv_b.reshape(DEPTH, 1, D_FF), w_down=w_down,
        final_g=final_g.reshape(1, D_MODEL),
    )
    last = DEPTH - 1
    n_seq = x_prompt.shape[0]
    zeros = lambda *shape: jnp.zeros((DEPTH,) + shape, F32)

    xp, xm = x_prompt, meta_tokens.reshape(1, N_META, D_MODEL)
    xs = x_sample.transpose(1, 0, 2).reshape(S_ROWS, D_MODEL)
    p_mix = [zeros(n_seq, POOL_BUF, D_POOL), zeros(n_seq, N_HEADS, HEAD, HEAD)]
    p_ffn = [zeros(n_seq, CONV_W - 1, D_FF)]
    s_mix = [zeros(POOL_BUF, DEC_BATCH, D_POOL), zeros(DEC_BATCH, N_HEADS, HEAD, HEAD)]
    pool_rows = state_pool.transpose(0, 2, 1, 3)
    s_ffn = [zeros(DEC_BATCH, CONV_W - 1, D_FF)]
    for l in range(DEPTH):
        xs, *s_mix = _sample_mixer(xs, p, l, pool_rows, state_rec, s_mix, name=f"sample_mixer_{l}")
        xs, conv_s, *ffn_weights = _sample_ffn(xs, p, l, state_conv, s_ffn, final=(l == last),
                                               name=f"sample_ffn_{l}")
        s_ffn = [conv_s]
        xp, xm, *p_mix = _prompt_mixer(xp, xm, p, l, p_mix, tile=PROMPT_TILE, sub=PROMPT_SUB,
                                       chunk=PROMPT_CHUNK, name=f"prompt_mixer_{l}")
        xp, xm, *p_ffn = _prompt_ffn(xp, xm, p, l, ffn_weights, p_ffn, tile=PROMPT_TILE, final=(l == last),
                                     name=f"prompt_ffn_{l}")
    pool_p, rec_p = p_mix
    conv_p, = p_ffn
    pool_s, rec_s = s_mix
    y_sample = xs.reshape(DEC_SEQ, DEC_BATCH, D_MODEL).transpose(1, 0, 2)

    return xp, y_sample, pool_p, rec_p, conv_p, pool_s.transpose(0, 2, 1, 3), rec_s, conv_s
```

```python
import functools
import itertools

import jax
import jax.numpy as jnp
from jax import lax
from jax.experimental import pallas as pl
from jax.experimental.pallas import tpu as pltpu

F32 = jnp.float32
BF16 = jnp.bfloat16

D_MODEL = 1024
DEPTH = 4
D_POOL = 512
D_REC = 512
N_HEADS = 4
HEAD = 128
N_PARTS = 4
D_IN = D_POOL + N_PARTS * D_REC
D_FF = 2816
POOL_WINDOWS = (2, 4, 8, 16)
POOL_GROUP = 128
POOL_BUF = 15
N_META = 16
DEC_BATCH = 128
DEC_SEQ = 4
CONV_W = 3
EPS = 1e-6
Q_SCALE = HEAD ** -0.5
SQRT_HALF = 0.7071067811865476

SUBLANES = 8
MXU_COLS = 256
HIST = 16
CONV_HIST = 8
VMEM_LIMIT = 56 * 1024 * 1024
SAMPLE_MIXER_VMEM_LIMIT = 62 * 1024 * 1024


def _rmsnorm(x, g):
    return x * lax.rsqrt(jnp.mean(x * x, axis=-1, keepdims=True) + EPS) * g


def _silu(x):
    return x * (1.0 / (1.0 + jnp.exp(-x)))


def _gelu(x):
    return 0.5 * x * (1.0 + lax.erf(x * SQRT_HALF))


def _lower_bound(lb_raw, layer):
    m = jnp.max(lb_raw, axis=0, keepdims=True)
    e = jnp.exp(lb_raw - m)
    p = e / jnp.sum(e, axis=0, keepdims=True)
    lb = jnp.zeros_like(m)
    for i in range(1, layer + 1):
        lb = lb + p[i:i + 1]
    return lb


def _forget_gate(f, lb):
    m = jnp.exp(-jnp.abs(f))
    r = 1.0 / (1.0 + m)
    sig_neg = jnp.where(f >= 0, m * r, r)
    log_sig = jnp.minimum(f, 0.0) - jnp.log(1.0 + m)
    a = jnp.log(lb)
    b = jnp.log(1.0 - lb) + log_sig
    log_f = jnp.maximum(a, b) + jnp.log(1.0 + jnp.exp(-jnp.abs(a - b)))
    return log_f, (1.0 - lb) * sig_neg


def _dot(a, b):
    return jnp.dot(a, b, preferred_element_type=F32)


def _dot_nt(a, b):
    return lax.dot_general(a, b, (((1,), (1,)), ((), ())), preferred_element_type=F32)


def _dot_tn(a, b):
    return lax.dot_general(a, b, (((0,), (0,)), ((), ())), preferred_element_type=F32)


HALF = SUBLANES // 2


def _gla_masks(n_blk):
    c_len = SUBLANES * n_blk
    lane = lax.broadcasted_iota(jnp.int32, (SUBLANES, c_len), 1)
    row = lax.broadcasted_iota(jnp.int32, (SUBLANES, c_len), 0)
    diag = lane - row
    near = [[diag == SUBLANES * i - r for r in range(HALF)] for i in range(n_blk)]
    across = [(row >= HALF) & (lane >= SUBLANES * i) & (lane < SUBLANES * i + HALF) for i in range(n_blk)]
    below = {c: lane < SUBLANES * c for c in range(1, n_blk)}
    return near, across, below


def _gla_intra(q, k, g, v, n_blk, masks):
    near, across, below = masks
    sub = lax.broadcasted_iota(jnp.int32, (SUBLANES, HEAD), 0)
    blk = lambda a, i: a[SUBLANES * i:SUBLANES * (i + 1)]
    square = lambda x: _dot_nt(x, x)

    local, tot = [], []
    for i in range(n_blk):
        x = blk(g, i)
        for s in (1, 2, 4):
            x = x + jnp.where(sub >= s, pltpu.roll(x, s, 0), 0.0)
        local.append(x)
        tot.append(jnp.exp(x[SUBLANES - 1:SUBLANES]))

    spans = {}

    def span(a, b):
        if a >= b:
            return None
        if (a, b) not in spans:
            if (a + 1, b) in spans:
                spans[a, b] = tot[a] * spans[a + 1, b]
            else:
                head = span(a, b - 1)
                spans[a, b] = tot[b - 1] if head is None else head * tot[b - 1]
        return spans[a, b]

    scaled = lambda x, s: x if s is None else x * s
    q_in = [blk(q, i) * jnp.exp(local[i]) for i in range(n_blk)]
    k_out = [blk(k, i) * jnp.exp(local[i][SUBLANES - 1:SUBLANES] - local[i]) for i in range(n_blk)]
    rows_bf16 = lambda tiles: jnp.concatenate(tiles, axis=0).astype(BF16)
    q_dec = rows_bf16([scaled(q_in[i], span(0, i)) for i in range(n_blk)])
    k_dec = rows_bf16([scaled(k_out[i], span(i + 1, n_blk)) for i in reversed(range(n_blk))][::-1])

    level_prod = {}
    half = n_blk // 2
    while half >= 1:
        tiles = {}
        for cut in range(half, n_blk, 2 * half):
            for i in range(cut, cut + half):
                tiles[i] = scaled(q_in[i], span(cut, i))
            for i in reversed(range(cut - half, cut)):
                tiles[i] = scaled(k_out[i], span(i + 1, cut))
        level_prod[half] = square(rows_bf16([tiles[i] for i in range(n_blk)]))
        half //= 2

    tiles = []
    for i in range(n_blk):
        mid = jnp.broadcast_to(local[i][HALF - 1:HALF], (SUBLANES, HEAD))
        tiles.append(jnp.where(sub >= HALF, blk(q, i), blk(k, i)) * jnp.exp(-jnp.abs(local[i] - mid)))
    half_prod = square(rows_bf16(tiles))

    rows = []
    for i in range(n_blk):
        ki = blk(k, i)
        gate = jnp.exp(blk(g, i))
        row = jnp.where(across[i], blk(half_prod, i), 0.0)
        w = blk(q, i)
        for r in range(HALF):
            if r > 0:
                w = w * (gate if r == 1 else pltpu.roll(gate, r - 1, 0))
            a = jnp.sum(w * (ki if r == 0 else pltpu.roll(ki, r, 0)), axis=-1, keepdims=True)
            row = jnp.where(near[i][r], a, row)
        half = 1
        while half < n_blk:
            if i % (2 * half) >= half:
                row = jnp.where(below[(i // (2 * half)) * 2 * half + half], blk(level_prod[half], i), row)
            half *= 2
        rows.append(row)

    return _dot(rows_bf16(rows), v.astype(BF16)), q_dec, k_dec, span(0, n_blk)


def _gla_state(intra, v, st):
    o_intra, q_dec, k_dec, decay = intra
    out = o_intra + _dot_nt(q_dec, st.astype(BF16))
    return out, st * decay + _dot_tn(v.astype(BF16), k_dec)


def _pool_count(start, offset, rows, w):
    if start >= max(POOL_WINDOWS) - 1:
        return 1.0 / w
    pos = start + offset + lax.broadcasted_iota(jnp.int32, (rows, 1), 0)
    return 1.0 / jnp.minimum(pos + 1, w).astype(F32)


def _rec_col(h, part):
    lo = D_POOL + part * D_REC + h * HEAD
    return slice(lo, lo + HEAD)


def _layer_spec(shape, layer):
    zeros = (0,) * len(shape)
    return pl.BlockSpec((None,) + tuple(shape), lambda *_: (layer,) + zeros, pipeline_mode=pl.Buffered(1))


def _const_spec(shape):
    zeros = (0,) * len(shape)
    return pl.BlockSpec(tuple(shape), lambda *_: zeros, pipeline_mode=pl.Buffered(1))


def _stacked(acc):
    return [pl.BlockSpec(memory_space=pl.ANY)] * len(acc), list(acc), len(acc)


def _prompt_mixer_kernel(*refs, n_alias, n_zero, layer, tile, sub, chunk):
    (x_ref, xn_ref, xm_ref, n1_ref, win_ref, wpool_ref, pscale_ref, lbraw_ref, gnorm_ref, wout_ref) = refs[:10]
    xo_ref, xmo_ref, poolo_ref, so_ref = refs[10 + n_alias:14 + n_alias]
    zero_refs = refs[14 + n_alias:14 + n_alias + n_zero]
    (z0_ref, z1_ref, mix0_ref, mix1_ref, u_ref, st_ref, hn_ref, pool0_ref, s0_ref) = refs[14 + n_alias + n_zero:]
    z_refs, mix_refs = (z0_ref, z1_ref), (mix0_ref, mix1_ref)
    b, t, n_tiles = pl.program_id(0), pl.program_id(1), pl.num_programs(1)
    n_sub = tile // sub
    unroll = 2
    lb = _lower_bound(lbraw_ref[...], layer)
    gnorm = gnorm_ref[...]

    def project(x, slot, rows, start, offset):
        z_ref, mix_ref = z_refs[slot], mix_refs[slot]

        def norm():
            hn_ref[0:rows, :] = _rmsnorm(x(), n1_ref[...]).astype(BF16)

        def columns(n):
            cols = slice(n * MXU_COLS, (n + 1) * MXU_COLS)

            def step():
                z_ref[0:rows, cols] = _dot(hn_ref[0:rows, :], win_ref[:, cols])
            return step

        def pool():
            u_ref[HIST:HIST + rows, :] = z_ref[0:rows, 0:D_POOL]
            for gi, w in enumerate(POOL_WINDOWS):
                lanes = slice(gi * POOL_GROUP, (gi + 1) * POOL_GROUP)
                acc = u_ref[HIST:HIST + rows, lanes]
                u = acc
                for j in range(1, w):
                    acc = acc + u_ref[HIST - j:HIST - j + rows, lanes]
                m = acc * _pool_count(start, offset, rows, w) - u
                y = _dot(m.astype(BF16), wpool_ref[gi]) * pscale_ref[:, lanes]
                mix_ref[0:rows, lanes] = y.astype(BF16)
            u_ref[0:HIST, :] = u_ref[rows:rows + HIST, :]

        n_pool = D_POOL // MXU_COLS
        tiles = [columns(n) for n in range(D_IN // MXU_COLS)]
        return [norm] + tiles[:n_pool] + [pool] + tiles[n_pool:]

    def attend(x, slot, rows, chunk, state_ref, store):
        z_ref, mix_ref = z_refs[slot], mix_refs[slot]
        n_blk, n_chunks = chunk // SUBLANES, rows // chunk
        intra = {}
        masks = _gla_masks(n_blk)

        def within(h, c):
            rows = slice(c * chunk, (c + 1) * chunk)

            def step():
                q = _silu(z_ref[rows, _rec_col(h, 0)]) * Q_SCALE
                log_f, k = _forget_gate(z_ref[rows, _rec_col(h, 1)], lb[:, h * HEAD:(h + 1) * HEAD])
                intra[h, c] = _gla_intra(q, k, log_f, z_ref[rows, _rec_col(h, 2)], n_blk, masks)
            return step

        state = {}

        def carry(h, c):
            rows = slice(c * chunk, (c + 1) * chunk)

            def step():
                st = state_ref[h] if c == 0 else state[h]
                o, state[h] = _gla_state(intra.pop((h, c)), z_ref[rows, _rec_col(h, 2)], st)
                o = o * lax.rsqrt(jnp.mean(o * o, axis=-1, keepdims=True) + EPS) * gnorm
                o = o * _silu(z_ref[rows, _rec_col(h, 3)])
                mix_ref[rows, D_POOL + h * HEAD:D_POOL + (h + 1) * HEAD] = o.astype(BF16)
                if c == n_chunks - 1:
                    state_ref[h] = state.pop(h)
            return step

        y = {}

        def out(lo, hi, last):
            def step():
                part = _dot(mix_ref[0:rows, lo:hi], wout_ref[lo:hi, :])
                y["acc"] = part if "acc" not in y else y["acc"] + part
                if last:
                    store(x() + y.pop("acc"))
            return step

        heavy, light = [], {}
        light[0] = [out(0, D_POOL, False)]
        for h in range(N_HEADS):
            for c in range(n_chunks):
                heavy.append(within(h, c))
                light[len(heavy)] = [carry(h, c)]
            if h % 2 == 1:
                lo = D_POOL + (h - 1) * HEAD
                light[len(heavy)].append(out(lo, lo + 2 * HEAD, h == N_HEADS - 1))
        return heavy, light

    def emit(filler, heavy=(), light=None):
        light = light or {}
        n = max(len(heavy), 1)
        done = 0
        for i in range(len(heavy) + 1):
            for step in light.get(i, ()):
                step()
            want = len(filler) if i >= len(heavy) else (len(filler) * (i + 1)) // n
            while done < want:
                filler[done]()
                done += 1
            if i < len(heavy):
                heavy[i]()

    def store_meta(y):
        xmo_ref[0] = y

    @pl.when((b == 0) & (t == 0))
    def _():
        u_ref[0:HIST, :] = jnp.zeros((HIST, D_POOL), F32)
        s0_ref[...] = jnp.zeros(s0_ref.shape, F32)
        xm = lambda: xm_ref[0]
        emit(project(xm, 0, N_META, 0, 0))
        pool0_ref[...] = u_ref[0:HIST, :]
        emit([], *attend(xm, 0, N_META, N_META, s0_ref, store_meta))
        emit(project(lambda: x_ref[0, 0:sub, :], 0, sub, N_META, 0))

    @pl.when(t == 0)
    def _():
        st_ref[...] = s0_ref[...]

    def trip(m, carry):
        for kk in range(unroll):
            i = m * unroll + kk
            r0 = pl.multiple_of(i * sub, sub)
            poolo_ref[0] = u_ref[HIST - POOL_BUF:HIST, :]
            if kk == unroll - 1:
                tile_end = i == n_sub - 1
                seq_end = tile_end & (t == n_tiles - 1)
                u_ref[0:HIST, :] = jnp.where(seq_end, pool0_ref[...], u_ref[0:HIST, :])
                rn = pl.multiple_of(jnp.minimum(i + 1, n_sub - 1) * sub, sub)

                def x_next(tile_end=tile_end, rn=rn):
                    return jnp.where(tile_end, xn_ref[0], x_ref[0, pl.ds(rn, sub), :])
            else:
                def x_next(r0=r0):
                    return x_ref[0, pl.ds(r0 + sub, sub), :]

            def x(r0=r0):
                return x_ref[0, pl.ds(r0, sub), :]

            def store(y, r0=r0):
                xo_ref[0, pl.ds(r0, sub), :] = y

            emit(project(x_next, (kk + 1) % 2, sub, N_META, 0), *attend(x, kk % 2, sub, chunk, st_ref, store))
        return carry

    lax.fori_loop(0, n_sub // unroll, trip, 0)

    for zero_ref in zero_refs:
        zero_ref[...] = jnp.zeros(zero_ref.shape, zero_ref.dtype)

    @pl.when(t == n_tiles - 1)
    def _():
        for h in range(N_HEADS):
            so_ref[0, h] = st_ref[h].T


def _prompt_mixer(x, xm, p, layer, acc, *, tile, sub, chunk, name, zero_shapes=()):
    n_seq, t_len, _ = x.shape
    n_tiles, n_sub = t_len // tile, tile // sub
    assert t_len % tile == 0 and n_sub % 2 == 0 and sub % chunk == 0
    alias_specs, alias_args, n_alias = _stacked(acc)
    n_in = 10

    def next_rows(b, t):
        nxt = jnp.minimum(b * n_tiles + t + 1, n_seq * n_tiles - 1)
        return nxt // n_tiles, (nxt % n_tiles) * n_sub, 0

    per_layer = n_seq * n_tiles // DEPTH
    assert n_seq * n_tiles == per_layer * DEPTH
    zero_specs = []
    for shape, axis in zero_shapes:
        assert shape[0] == DEPTH and shape[axis] % per_layer == 0
        block = list(shape)
        block[0], block[axis] = 1, shape[axis] // per_layer

        def index(b, t, axis=axis, rank=len(shape)):
            step = b * n_tiles + t
            idx = [0] * rank
            idx[0], idx[axis] = step // per_layer, step % per_layer
            return tuple(idx)

        zero_specs.append(pl.BlockSpec(tuple(block), index))
    kern = functools.partial(_prompt_mixer_kernel, n_alias=n_alias, n_zero=len(zero_shapes), layer=layer,
                             tile=tile, sub=sub, chunk=chunk)
    return pl.pallas_call(
        kern,
        grid=(n_seq, n_tiles),
        in_specs=[
            pl.BlockSpec((1, tile, D_MODEL), lambda b, t: (b, t, 0)),
            pl.BlockSpec((1, sub, D_MODEL), next_rows),
            _const_spec((1, N_META, D_MODEL)),
            _layer_spec((1, D_MODEL), layer),
            _layer_spec((D_MODEL, D_IN), layer),
            _layer_spec((len(POOL_WINDOWS), POOL_GROUP, POOL_GROUP), layer),
            _layer_spec((1, D_POOL), layer),
            _const_spec((DEPTH, D_REC)),
            _layer_spec((1, HEAD), layer),
            _layer_spec((D_MODEL, D_MODEL), layer),
        ] + alias_specs,
        out_specs=[
            pl.BlockSpec((1, tile, D_MODEL), lambda b, t: (b, t, 0)),
            pl.BlockSpec((1, N_META, D_MODEL), lambda b, t: (0, 0, 0)),
            pl.BlockSpec((None, 1, POOL_BUF, D_POOL), lambda b, t: (layer, b, 0, 0)),
            pl.BlockSpec((None, 1, N_HEADS, HEAD, HEAD), lambda b, t: (layer, b, 0, 0, 0)),
        ] + zero_specs,
        out_shape=[
            jax.ShapeDtypeStruct((n_seq, t_len, D_MODEL), F32),
            jax.ShapeDtypeStruct((1, N_META, D_MODEL), F32),
            jax.ShapeDtypeStruct((DEPTH, n_seq, POOL_BUF, D_POOL), F32),
            jax.ShapeDtypeStruct((DEPTH, n_seq, N_HEADS, HEAD, HEAD), F32),
        ] + [jax.ShapeDtypeStruct(shape, F32) for shape, _ in zero_shapes],
        scratch_shapes=[
            pltpu.VMEM((sub, D_IN), F32),
            pltpu.VMEM((sub, D_IN), F32),
            pltpu.VMEM((sub, D_MODEL), BF16),
            pltpu.VMEM((sub, D_MODEL), BF16),
            pltpu.VMEM((HIST + sub, D_POOL), F32),
            pltpu.VMEM((N_HEADS, HEAD, HEAD), F32),
            pltpu.VMEM((sub, D_MODEL), BF16),
            pltpu.VMEM((HIST, D_POOL), F32),
            pltpu.VMEM((N_HEADS, HEAD, HEAD), F32),
        ],
        input_output_aliases={n_in + k: 2 + k for k in range(n_alias)},
        compiler_params=pltpu.CompilerParams(
            dimension_semantics=("arbitrary", "arbitrary"), vmem_limit_bytes=VMEM_LIMIT),
        name=name,
    )(x, x, xm, p["n1"], p["w_in"], p["w_pool"], p["pool_scale"], p["lb_raw"], p["g_norm"], p["w_out"],
      *alias_args)


def _prompt_ffn_kernel(*refs, n_alias, tile, final):
    x_ref, xm_ref, n2_ref, wa_ref, wb_ref, cw_ref, cb_ref, wd_ref, fg_ref = refs[:9]
    xo_ref, xmo_ref, convo_ref, a_ref, conv0_ref = refs[9 + n_alias:]
    b, t = pl.program_id(0), pl.program_id(1)
    lo = CONV_HIST - (CONV_W - 1)

    def ffn(x, rows):
        hn = _rmsnorm(x, n2_ref[...]).astype(BF16)
        a_ref[CONV_HIST:CONV_HIST + rows, :] = _dot(hn, wa_ref[...])
        gate = _dot(hn, wb_ref[...])
        ac = cb_ref[...]
        for j in range(CONV_W):
            ac = ac + cw_ref[j:j + 1, :] * a_ref[lo + j:lo + j + rows, :]
        act = (_gelu(ac) * gate).astype(BF16)
        return x + _dot(act, wd_ref[...])

    @pl.when((b == 0) & (t == 0))
    def _():
        a_ref[lo:CONV_HIST, :] = jnp.zeros((CONV_W - 1, D_FF), F32)
        xmo_ref[0] = ffn(xm_ref[0], N_META)
        conv0_ref[...] = a_ref[lo + N_META:CONV_HIST + N_META, :]

    @pl.when(t == 0)
    def _():
        a_ref[lo:CONV_HIST, :] = conv0_ref[...]

    xo = ffn(x_ref[0], tile)
    if final:
        xo = _rmsnorm(xo, fg_ref[...])
    xo_ref[0] = xo

    @pl.when(t == pl.num_programs(1) - 1)
    def _():
        convo_ref[0] = a_ref[lo + tile:CONV_HIST + tile, :]

    a_ref[lo:CONV_HIST, :] = a_ref[lo + tile:CONV_HIST + tile, :]


def _prompt_ffn(x, xm, p, layer, weights, acc, *, tile, final, name):
    n_seq, t_len, _ = x.shape
    w_a, w_b, w_down = weights
    alias_specs, alias_args, n_alias = _stacked(acc)
    n_in = 9
    kern = functools.partial(_prompt_ffn_kernel, n_alias=n_alias, tile=tile, final=final)
    return pl.pallas_call(
        kern,
        grid=(n_seq, t_len // tile),
        in_specs=[
            pl.BlockSpec((1, tile, D_MODEL), lambda b, t: (b, t, 0)),
            _const_spec((1, N_META, D_MODEL)),
            _layer_spec((1, D_MODEL), layer),
            _const_spec((D_MODEL, D_FF)),
            _const_spec((D_MODEL, D_FF)),
            _layer_spec((CONV_W, D_FF), layer),
            _layer_spec((1, D_FF), layer),
            _const_spec((D_FF, D_MODEL)),
            _const_spec((1, D_MODEL)),
        ] + alias_specs,
        out_specs=[
            pl.BlockSpec((1, tile, D_MODEL), lambda b, t: (b, t, 0)),
            pl.BlockSpec((1, N_META, D_MODEL), lambda b, t: (0, 0, 0)),
            pl.BlockSpec((None, 1, CONV_W - 1, D_FF), lambda b, t: (layer, b, 0, 0)),
        ],
        out_shape=[
            jax.ShapeDtypeStruct((n_seq, t_len, D_MODEL), F32),
            jax.ShapeDtypeStruct((1, N_META, D_MODEL), F32),
            jax.ShapeDtypeStruct((DEPTH, n_seq, CONV_W - 1, D_FF), F32),
        ],
        scratch_shapes=[
            pltpu.VMEM((CONV_HIST + tile, D_FF), F32),
            pltpu.VMEM((CONV_W - 1, D_FF), F32),
        ],
        input_output_aliases={n_in + k: 2 + k for k in range(n_alias)},
        compiler_params=pltpu.CompilerParams(
            dimension_semantics=("arbitrary", "arbitrary"), vmem_limit_bytes=VMEM_LIMIT),
        name=name,
    )(x, xm, p["n2"], w_a, w_b, p["conv_w"], p["conv_b"], w_down, p["final_g"], *alias_args)


S_ROWS = DEC_BATCH * DEC_SEQ
S_SLOT = SUBLANES * DEC_BATCH
N_SRC = DEC_SEQ + 3
SEQ_BLOCK = 16


def _sample_mixer_kernel(*refs, n_alias, layer):
    (x_ref, n1_ref, win_ref, wpool_ref, pscale_ref, lbraw_ref, gnorm_ref, wout_ref, sp_ref, s_ref) = refs[:10]
    (xo_ref, spo_ref, so_ref, z_ref, mix_ref, oin_ref, src_ref, gather_ref, scatter_ref,
     qd8_ref, kd8_ref, v8_ref, o8_ref) = refs[10 + n_alias:]
    step = pl.program_id(0)
    tok = lambda t: slice(t * DEC_BATCH, (t + 1) * DEC_BATCH)
    rec = lambda part: [_rec_col(h, part) for h in range(N_HEADS)]

    def heads(rows, part):
        return jnp.concatenate([z_ref[rows, c] for c in rec(part)], axis=1)

    @pl.when(step == 0)
    def _():
        hn = _rmsnorm(x_ref[...], n1_ref[...]).astype(BF16)
        z_ref[...] = _dot(hn, win_ref[...])

        def ext(j, lanes):
            if j >= POOL_BUF:
                return z_ref[tok(j - POOL_BUF), lanes]
            return sp_ref[j, :, lanes]

        for gi, w in enumerate(POOL_WINDOWS):
            lanes = slice(gi * POOL_GROUP, (gi + 1) * POOL_GROUP)
            ms = []
            for t in range(DEC_SEQ):
                acc = ext(POOL_BUF + t, lanes)
                u = acc
                for j in range(1, w):
                    acc = acc + ext(POOL_BUF + t - j, lanes)
                ms.append(acc * (1.0 / w) - u)
            y = _dot(jnp.concatenate(ms, axis=0).astype(BF16), wpool_ref[gi]) * pscale_ref[:, lanes]
            mix_ref[:, lanes] = y
        full = slice(0, D_POOL)
        for j in range(POOL_BUF):
            spo_ref[j] = ext(j + DEC_SEQ, full)

        lb = _lower_bound(lbraw_ref[...], layer)
        qs, ks, gs, vs = [], [], [], []
        for t in range(DEC_SEQ):
            qs.append(_silu(heads(tok(t), 0)) * Q_SCALE)
            log_f, k = _forget_gate(heads(tok(t), 1), lb)
            ks.append(k)
            gs.append(log_f if t == 0 else gs[-1] + log_f)
            vs.append(heads(tok(t), 2))
        g_last = gs[-1]
        for t in range(DEC_SEQ):
            acc = None
            for s in range(t + 1):
                prod = qs[t] * ks[s]
                if s < t:
                    prod = prod * jnp.exp(gs[t] - gs[s])
                parts = []
                for h in range(N_HEADS):
                    hl = slice(h * HEAD, (h + 1) * HEAD)
                    a = jnp.sum(prod[:, hl], axis=-1, keepdims=True)
                    parts.append(a * vs[s][:, hl])
                term = jnp.concatenate(parts, axis=1)
                acc = term if acc is None else acc + term
            oin_ref[tok(t), :] = acc

        row = lax.broadcasted_iota(jnp.int32, (S_SLOT, N_SRC * DEC_BATCH), 0)
        col = lax.broadcasted_iota(jnp.int32, (S_SLOT, N_SRC * DEC_BATCH), 1)
        hit = ((row & (SUBLANES - 1)) == (col >> 7)) & ((row >> 3) == (col & (DEC_BATCH - 1)))
        gather_ref[...] = jnp.where(hit, 1.0, 0.0).astype(BF16)
        row = lax.broadcasted_iota(jnp.int32, (S_ROWS, S_SLOT), 0)
        col = lax.broadcasted_iota(jnp.int32, (S_ROWS, S_SLOT), 1)
        hit = ((col & (SUBLANES - 1)) == (row >> 7)) & ((col >> 3) == (row & (DEC_BATCH - 1)))
        scatter_ref[...] = jnp.where(hit, 1.0, 0.0).astype(BF16)

        for t in range(DEC_SEQ):
            src_ref[tok(t), :] = (ks[t] * jnp.exp(g_last - gs[t])).astype(BF16)
        rest = jnp.exp(g_last)
        for part in range(3):
            piece = rest.astype(BF16)
            src_ref[tok(DEC_SEQ + part), :] = piece
            rest = rest - piece.astype(F32)
        kd8_ref[...] = _dot(gather_ref[...], src_ref[...])
        for t in range(DEC_SEQ):
            src_ref[tok(t), :] = (qs[t] * jnp.exp(gs[t])).astype(BF16)
        qd8_ref[...] = _dot(gather_ref[:, 0:S_ROWS], src_ref[0:S_ROWS, :])
        for t in range(DEC_SEQ):
            src_ref[tok(t), :] = vs[t].astype(BF16)
        v8_ref[...] = _dot(gather_ref[:, 0:S_ROWS], src_ref[0:S_ROWS, :])

    slot = lax.broadcasted_iota(jnp.int32, (SUBLANES, HEAD), 0)
    ones8 = jnp.where((slot >= DEC_SEQ) & (slot < N_SRC), 1.0, 0.0).astype(BF16)
    base = pl.multiple_of(step * (SEQ_BLOCK * SUBLANES), SEQ_BLOCK * SUBLANES)
    for j in range(SEQ_BLOCK):
        rows = pl.ds(base + j * SUBLANES, SUBLANES)
        for h in range(N_HEADS):
            hl = slice(h * HEAD, (h + 1) * HEAD)
            st = s_ref[j, h]
            kd = kd8_ref[rows, hl].astype(BF16)
            o8_ref[rows, hl] = _dot(qd8_ref[rows, hl].astype(BF16), st.astype(BF16))
            so_ref[j, h] = _dot_tn(kd, ones8) * st + _dot_tn(kd, v8_ref[rows, hl].astype(BF16))

    @pl.when(step == pl.num_programs(0) - 1)
    def _():
        o8 = o8_ref[...]
        hi = o8.astype(BF16)
        lo = (o8 - hi.astype(F32)).astype(BF16)
        o = oin_ref[...] + _dot(scatter_ref[...], hi) + _dot(scatter_ref[...], lo)
        for h in range(N_HEADS):
            hl = slice(h * HEAD, (h + 1) * HEAD)
            oh = o[:, hl]
            oh = oh * lax.rsqrt(jnp.mean(oh * oh, axis=-1, keepdims=True) + EPS) * gnorm_ref[...]
            mix_ref[:, D_POOL + h * HEAD:D_POOL + (h + 1) * HEAD] = oh * _silu(z_ref[:, _rec_col(h, 3)])
        xo_ref[...] = x_ref[...] + _dot(mix_ref[...].astype(BF16), wout_ref[...])


def _sample_mixer(x, p, layer, state_pool, state_rec, acc, *, name):
    alias_specs, alias_args, n_alias = _stacked(acc)
    n_in = 10
    kern = functools.partial(_sample_mixer_kernel, n_alias=n_alias, layer=layer)
    rec_block = (None, SEQ_BLOCK, N_HEADS, HEAD, HEAD)
    return pl.pallas_call(
        kern,
        grid=(DEC_BATCH // SEQ_BLOCK,),
        in_specs=[
            _const_spec((S_ROWS, D_MODEL)),
            _layer_spec((1, D_MODEL), layer),
            _layer_spec((D_MODEL, D_IN), layer),
            _layer_spec((len(POOL_WINDOWS), POOL_GROUP, POOL_GROUP), layer),
            _layer_spec((1, D_POOL), layer),
            _const_spec((DEPTH, D_REC)),
            _layer_spec((1, HEAD), layer),
            _layer_spec((D_MODEL, D_MODEL), layer),
            _layer_spec((POOL_BUF, DEC_BATCH, D_POOL), layer),
            pl.BlockSpec(rec_block, lambda s: (layer, s, 0, 0, 0)),
        ] + alias_specs,
        out_specs=[
            pl.BlockSpec((S_ROWS, D_MODEL), lambda s: (0, 0)),
            pl.BlockSpec((None, POOL_BUF, DEC_BATCH, D_POOL), lambda s: (layer, 0, 0, 0)),
            pl.BlockSpec(rec_block, lambda s: (layer, s, 0, 0, 0)),
        ],
        out_shape=[
            jax.ShapeDtypeStruct((S_ROWS, D_MODEL), F32),
            jax.ShapeDtypeStruct((DEPTH, POOL_BUF, DEC_BATCH, D_POOL), F32),
            jax.ShapeDtypeStruct((DEPTH, DEC_BATCH, N_HEADS, HEAD, HEAD), F32),
        ],
        scratch_shapes=[
            pltpu.VMEM((S_ROWS, D_IN), F32),
            pltpu.VMEM((S_ROWS, D_MODEL), F32),
            pltpu.VMEM((S_ROWS, D_REC), F32),
            pltpu.VMEM((N_SRC * DEC_BATCH, D_REC), BF16),
            pltpu.VMEM((S_SLOT, N_SRC * DEC_BATCH), BF16),
            pltpu.VMEM((S_ROWS, S_SLOT), BF16),
            pltpu.VMEM((S_SLOT, D_REC), F32),
            pltpu.VMEM((S_SLOT, D_REC), F32),
            pltpu.VMEM((S_SLOT, D_REC), F32),
            pltpu.VMEM((S_SLOT, D_REC), F32),
        ],
        input_output_aliases={n_in + k: 1 + k for k in range(n_alias)},
        compiler_params=pltpu.CompilerParams(
            dimension_semantics=("arbitrary",), vmem_limit_bytes=SAMPLE_MIXER_VMEM_LIMIT),
        name=name,
    )(x, p["n1"], p["w_in"], p["w_pool"], p["pool_scale"], p["lb_raw"], p["g_norm"], p["w_out"],
      state_pool, state_rec, *alias_args)


FFN_BLOCK = MXU_COLS


def _sample_ffn_kernel(*refs, n_alias, final):
    x_ref, n2_ref, wa_ref, wb_ref, cw_ref, cb_ref, wd_ref, cs_ref, fg_ref = refs[:9]
    xo_ref, cso_ref, wao_ref, wbo_ref, wdo_ref, hn_ref, acc_ref = refs[9 + n_alias:]
    step = pl.program_id(0)
    tok = lambda t: slice(t * DEC_BATCH, (t + 1) * DEC_BATCH)

    @pl.when(step == 0)
    def _():
        hn_ref[...] = _rmsnorm(x_ref[...], n2_ref[...]).astype(BF16)
        acc_ref[...] = jnp.zeros(acc_ref.shape, F32)

    wa, wb, wd = wa_ref[...].astype(BF16), wb_ref[...].astype(BF16), wd_ref[...].astype(BF16)
    wao_ref[...] = wa
    wbo_ref[...] = wb
    wdo_ref[...] = wd
    hn = hn_ref[...]
    a = _dot(hn, wa)
    gate = _dot(hn, wb)
    ext = [cs_ref[:, j, :] for j in range(CONV_W - 1)] + [a[tok(t)] for t in range(DEC_SEQ)]
    acs = []
    for t in range(DEC_SEQ):
        ac = cb_ref[...]
        for j in range(CONV_W):
            ac = ac + cw_ref[j:j + 1, :] * ext[t + j]
        acs.append(ac)
    act = (_gelu(jnp.concatenate(acs, axis=0)) * gate).astype(BF16)
    acc_ref[...] += _dot(act, wd)
    for j in range(CONV_W - 1):
        cso_ref[:, j, :] = ext[DEC_SEQ + j]

    @pl.when(step == pl.num_programs(0) - 1)
    def _():
        xo = x_ref[...] + acc_ref[...]
        if final:
            xo = _rmsnorm(xo, fg_ref[...])
        xo_ref[...] = xo


def _sample_ffn(x, p, layer, state_conv, acc, *, final, name):
    alias_specs, alias_args, n_alias = _stacked(acc)
    n_in = 9
    kern = functools.partial(_sample_ffn_kernel, n_alias=n_alias, final=final)
    cols = lambda shape: pl.BlockSpec((None,) + shape, lambda j: (layer,) + (0,) * (len(shape) - 1) + (j,))
    return pl.pallas_call(
        kern,
        grid=(D_FF // FFN_BLOCK,),
        in_specs=[
            _const_spec((S_ROWS, D_MODEL)),
            _layer_spec((1, D_MODEL), layer),
            cols((D_MODEL, FFN_BLOCK)),
            cols((D_MODEL, FFN_BLOCK)),
            cols((CONV_W, FFN_BLOCK)),
            cols((1, FFN_BLOCK)),
            pl.BlockSpec((None, FFN_BLOCK, D_MODEL), lambda j: (layer, j, 0)),
            cols((DEC_BATCH, CONV_W - 1, FFN_BLOCK)),
            _const_spec((1, D_MODEL)),
        ] + alias_specs,
        out_specs=[
            pl.BlockSpec((S_ROWS, D_MODEL), lambda j: (0, 0)),
            cols((DEC_BATCH, CONV_W - 1, FFN_BLOCK)),
            pl.BlockSpec((D_MODEL, FFN_BLOCK), lambda j: (0, j)),
            pl.BlockSpec((D_MODEL, FFN_BLOCK), lambda j: (0, j)),
            pl.BlockSpec((FFN_BLOCK, D_MODEL), lambda j: (j, 0)),
        ],
        out_shape=[
            jax.ShapeDtypeStruct((S_ROWS, D_MODEL), F32),
            jax.ShapeDtypeStruct((DEPTH, DEC_BATCH, CONV_W - 1, D_FF), F32),
            jax.ShapeDtypeStruct((D_MODEL, D_FF), BF16),
            jax.ShapeDtypeStruct((D_MODEL, D_FF), BF16),
            jax.ShapeDtypeStruct((D_FF, D_MODEL), BF16),
        ],
        scratch_shapes=[
            pltpu.VMEM((S_ROWS, D_MODEL), BF16),
            pltpu.VMEM((S_ROWS, D_MODEL), F32),
        ],
        input_output_aliases={n_in + k: 1 + k for k in range(n_alias)},
        compiler_params=pltpu.CompilerParams(
            dimension_semantics=("arbitrary",), vmem_limit_bytes=VMEM_LIMIT),
        name=name,
    )(x, p["n2"], p["w_a"], p["w_b"], p["conv_w"], p["conv_b"], p["w_down"], state_conv, p["final_g"],
      *alias_args)


MIXER_TILE = 1024
MIXER_SUB = 256
MIXER_CHUNK = 64
FFN_TILE = 512


def kernel(x_prompt, x_sample, state_pool, state_rec, state_conv, meta_tokens, norm1_g, w_in, w_pool, pool_scale, lb_raw, rec_norm_g, w_out, norm2_g, w_a, w_b, conv_w, conv_b, w_down, final_g):
    p = dict(
        n1=norm1_g.reshape(DEPTH, 1, D_MODEL), w_in=w_in.astype(BF16),
        w_pool=w_pool.astype(BF16), pool_scale=pool_scale.reshape(DEPTH, 1, D_POOL), lb_raw=lb_raw,
        g_norm=rec_norm_g.reshape(DEPTH, 1, HEAD), w_out=w_out.astype(BF16),
        n2=norm2_g.reshape(DEPTH, 1, D_MODEL), w_a=w_a, w_b=w_b,
        conv_w=conv_w, conv_b=conv_b.reshape(DEPTH, 1, D_FF), w_down=w_down,
        final_g=final_g.reshape(1, D_MODEL),
    )
    last = DEPTH - 1
    n_seq = x_prompt.shape[0]
    zeros = lambda *shape: jnp.zeros((DEPTH,) + shape, F32)

    xp, xm = x_prompt, meta_tokens.reshape(1, N_META, D_MODEL)
    xs = x_sample.transpose(1, 0, 2).reshape(S_ROWS, D_MODEL)
    pool_rows = state_pool.transpose(0, 2, 1, 3)
    p_mix = [zeros(n_seq, POOL_BUF, D_POOL), zeros(n_seq, N_HEADS, HEAD, HEAD)]
    p_ffn = [zeros(n_seq, CONV_W - 1, D_FF)]
    sample_results = [((DEPTH, POOL_BUF, DEC_BATCH, D_POOL), 2), ((DEPTH, DEC_BATCH, N_HEADS, HEAD, HEAD), 1),
                      ((DEPTH, DEC_BATCH, CONV_W - 1, D_FF), 1)]
    for l in range(DEPTH):
        xp, xm, *p_mix = _prompt_mixer(xp, xm, p, l, p_mix, tile=MIXER_TILE, sub=MIXER_SUB, chunk=MIXER_CHUNK,
                                       name=f"prompt_mixer_{l}", zero_shapes=sample_results if l == 0 else ())
        if l == 0:
            p_mix, (pool_s, rec_s, conv_s) = p_mix[:2], p_mix[2:]
            s_mix, s_ffn = [pool_s, rec_s], [conv_s]
        xs, *s_mix = _sample_mixer(xs, p, l, pool_rows, state_rec, s_mix, name=f"sample_mixer_{l}")
        xs, conv_s, *ffn_weights = _sample_ffn(xs, p, l, state_conv, s_ffn, final=(l == last),
                                               name=f"sample_ffn_{l}")
        s_ffn = [conv_s]
        xp, xm, *p_ffn = _prompt_ffn(xp, xm, p, l, ffn_weights, p_ffn, tile=FFN_TILE, final=(l == last),
                                     name=f"prompt_ffn_{l}")
    pool_p, rec_p = p_mix
    conv_p, = p_ffn
    pool_s, rec_s = s_mix
    y_sample = xs.reshape(DEC_SEQ, DEC_BATCH, D_MODEL).transpose(1, 0, 2)

    return xp, y_sample, pool_p, rec_p, conv_p, pool_s.transpose(0, 2, 1, 3), rec_s, conv_s
```

```python
import functools

import jax
import jax.numpy as jnp
from jax import lax
from jax.experimental import pallas as pl
from jax.experimental.pallas import tpu as pltpu

F32 = jnp.float32
BF16 = jnp.bfloat16

D_MODEL = 1024
DEPTH = 4
D_POOL = 512
D_REC = 512
N_HEADS = 4
HEAD = 128
N_PARTS = 4
D_IN = D_POOL + N_PARTS * D_REC
D_FF = 2816
POOL_WINDOWS = (2, 4, 8, 16)
POOL_GROUP = 128
POOL_BUF = 15
N_META = 16
DEC_BATCH = 128
DEC_SEQ = 4
CONV_W = 3
EPS = 1e-6
Q_SCALE = HEAD ** -0.5
SQRT_HALF = 0.7071067811865476

SUBLANES = 8
BF16_ROWS = 16
MXU_COLS = 256
INTRA_ROUNDS_PER_FILLER = 9
HIST = 16
CONV_HIST = 8
VMEM_LIMIT = 56 * 1024 * 1024
BIG_VMEM_LIMIT = 62 * 1024 * 1024


def _rmsnorm(x, g):
    return x * lax.rsqrt(jnp.mean(x * x, axis=-1, keepdims=True) + EPS) * g


def _silu(x):
    return x * (1.0 / (1.0 + jnp.exp(-x)))


def _gelu(x):
    return 0.5 * x * (1.0 + lax.erf(x * SQRT_HALF))


def _lower_bound(lb_raw, layer):
    m = jnp.max(lb_raw, axis=0, keepdims=True)
    e = jnp.exp(lb_raw - m)
    p = e / jnp.sum(e, axis=0, keepdims=True)
    lb = jnp.zeros_like(m)
    for i in range(1, layer + 1):
        lb = lb + p[i:i + 1]
    return lb


def _forget_gate(f, lb):
    m = jnp.exp(-jnp.abs(f))
    r = 1.0 / (1.0 + m)
    sig_neg = jnp.where(f >= 0, m * r, r)
    log_sig = jnp.minimum(f, 0.0) - jnp.log(1.0 + m)
    a = jnp.log(lb)
    b = jnp.log(1.0 - lb) + log_sig
    log_f = jnp.maximum(a, b) + jnp.log(1.0 + jnp.exp(-jnp.abs(a - b)))
    return log_f, (1.0 - lb) * sig_neg


def _dot(a, b):
    return jnp.dot(a, b, preferred_element_type=F32)


def _dot_nt(a, b):
    return lax.dot_general(a, b, (((1,), (1,)), ((), ())), preferred_element_type=F32)


def _dot_tn(a, b):
    return lax.dot_general(a, b, (((0,), (0,)), ((), ())), preferred_element_type=F32)


HALF = SUBLANES // 2


def _gla_masks(n_blk):
    c_len = SUBLANES * n_blk
    lane = lax.broadcasted_iota(jnp.int32, (SUBLANES, c_len), 1)
    row = lax.broadcasted_iota(jnp.int32, (SUBLANES, c_len), 0)
    diag = lane - row
    near = [[diag == SUBLANES * i - r for r in range(HALF)] for i in range(n_blk)]
    band = [(diag <= SUBLANES * i) & (diag > SUBLANES * i - HALF) for i in range(n_blk)]
    across = [(row >= HALF) & (lane >= SUBLANES * i) & (lane < SUBLANES * i + HALF) for i in range(n_blk)]
    below = {c: lane < SUBLANES * c for c in range(1, n_blk)}
    return near, band, across, below


def _gla_intra(q_raw, f_raw, lb, v, n_blk, masks):
    near, band, across, below = masks
    c_len = SUBLANES * n_blk
    width = q_raw.shape[1]
    heads = [slice(lo, lo + HEAD) for lo in range(0, width, HEAD)]
    sub = lax.broadcasted_iota(jnp.int32, (SUBLANES, width), 0)
    blk = lambda a, i: a[SUBLANES * i:SUBLANES * (i + 1)]
    square = lambda x: _dot_nt(x, x)
    rows_bf16 = lambda tiles: jnp.concatenate(tiles, axis=0).astype(BF16)

    tot, q_in, k_out, halves, near_rows = [], [], [], [], []
    for i in range(n_blk):
        qi = _silu(blk(q_raw, i)) * Q_SCALE
        gi, ki = _forget_gate(blk(f_raw, i), lb)
        yield
        local = gi
        for s in (1, 2, 4):
            local = local + jnp.where(sub >= s, pltpu.roll(local, s, 0), 0.0)
        yield
        last = local[SUBLANES - 1:SUBLANES]
        tot.append(jnp.exp(last))
        q_in.append(qi * jnp.exp(local))
        k_out.append(ki * jnp.exp(last - local))
        mid = jnp.broadcast_to(local[HALF - 1:HALF], (SUBLANES, width))
        halves.append(jnp.where(sub >= HALF, qi, ki) * jnp.exp(-jnp.abs(local - mid)))
        yield
        gate = jnp.exp(gi)
        w, rows = qi, [jnp.zeros((SUBLANES, c_len), F32) for _ in heads]
        for r in range(HALF):
            if r > 0:
                w = w * (gate if r == 1 else pltpu.roll(gate, r - 1, 0))
            pair = w * (ki if r == 0 else pltpu.roll(ki, r, 0))
            for n, hl in enumerate(heads):
                rows[n] = jnp.where(near[i][r], jnp.sum(pair[:, hl], axis=-1, keepdims=True), rows[n])
            yield
        near_rows.append(rows)

    spans = {}

    def span(a, b):
        if a >= b:
            return None
        if (a, b) not in spans:
            if (a + 1, b) in spans:
                spans[a, b] = tot[a] * spans[a + 1, b]
            else:
                head = span(a, b - 1)
                spans[a, b] = tot[b - 1] if head is None else head * tot[b - 1]
        return spans[a, b]

    scaled = lambda x, s: x if s is None else x * s
    q_dec = rows_bf16([scaled(q_in[i], span(0, i)) for i in range(n_blk)])
    k_dec = rows_bf16([scaled(k_out[i], span(i + 1, n_blk)) for i in reversed(range(n_blk))][::-1])
    yield

    level_prod, level_row = {}, {}
    half = n_blk // 2
    while half >= 1:
        tiles, upper = {}, []
        for cut in range(half, n_blk, 2 * half):
            for i in range(cut, cut + half):
                tiles[i] = scaled(q_in[i], span(cut, i))
                upper.append(i)
            for i in reversed(range(cut - half, cut)):
                tiles[i] = scaled(k_out[i], span(i + 1, cut))
        both = rows_bf16([tiles[i] for i in range(n_blk)])
        if (len(upper) * SUBLANES) % BF16_ROWS:
            upper = list(range(n_blk))
        left = both if len(upper) == n_blk else rows_bf16([tiles[i] for i in upper])
        level_prod[half] = [_dot_nt(left[:, hl], both[:, hl]) for hl in heads]
        level_row[half] = {i: n for n, i in enumerate(upper)}
        half //= 2
        yield
    halves = rows_bf16(halves)
    half_prod = [square(halves[:, hl]) for hl in heads]
    yield

    v_b, decay, out = v.astype(BF16), span(0, n_blk), []
    for n, hl in enumerate(heads):
        rows = []
        for i in range(n_blk):
            row = jnp.where(band[i], near_rows[i][n], jnp.where(across[i], blk(half_prod[n], i), 0.0))
            half = 1
            while half < n_blk:
                if i % (2 * half) >= half:
                    prod = blk(level_prod[half][n], level_row[half][i])
                    row = jnp.where(below[(i // (2 * half)) * 2 * half + half], prod, row)
                half *= 2
            rows.append(row)
        out.append((_dot(rows_bf16(rows), v_b[:, hl]), q_dec[:, hl], k_dec[:, hl], decay[:, hl]))
        yield
    return out


def _gla_state(intra, v, st):
    o_intra, q_dec, k_dec, decay = intra
    out = o_intra + _dot_nt(q_dec, st.astype(BF16))
    return out, st * decay + _dot_tn(v.astype(BF16), k_dec)


def _pool_count(start, offset, rows, w):
    if start >= max(POOL_WINDOWS) - 1:
        return 1.0 / w
    pos = start + offset + lax.broadcasted_iota(jnp.int32, (rows, 1), 0)
    return 1.0 / jnp.minimum(pos + 1, w).astype(F32)


def _rec_col(h, part):
    lo = D_POOL + part * D_REC + h * HEAD
    return slice(lo, lo + HEAD)


def _layer_spec(shape, layer):
    zeros = (0,) * len(shape)
    return pl.BlockSpec((None,) + tuple(shape), lambda *_: (layer,) + zeros, pipeline_mode=pl.Buffered(1))


def _const_spec(shape):
    zeros = (0,) * len(shape)
    return pl.BlockSpec(tuple(shape), lambda *_: zeros, pipeline_mode=pl.Buffered(1))


def _stacked(acc):
    return [pl.BlockSpec(memory_space=pl.ANY)] * len(acc), list(acc), len(acc)


def _prompt_mixer_kernel(*refs, n_alias, n_zero, layer, tile, sub, chunk):
    (x_ref, xn_ref, xm_ref, n1_ref, win_ref, wpool_ref, pscale_ref, lbraw_ref, gnorm_ref, wout_ref) = refs[:10]
    xo_ref, xmo_ref, poolo_ref, so_ref = refs[10 + n_alias:14 + n_alias]
    zero_refs = refs[14 + n_alias:14 + n_alias + n_zero]
    (z0_ref, z1_ref, mix0_ref, mix1_ref, u_ref, st_ref, hn_ref, pool0_ref, s0_ref) = refs[14 + n_alias + n_zero:]
    z_refs, mix_refs = (z0_ref, z1_ref), (mix0_ref, mix1_ref)
    b, t, n_tiles = pl.program_id(0), pl.program_id(1), pl.num_programs(1)
    n_sub = tile // sub
    unroll = 2
    lb = _lower_bound(lbraw_ref[...], layer)
    gnorm = gnorm_ref[...]

    def project(x, slot, rows, start, offset):
        z_ref, mix_ref = z_refs[slot], mix_refs[slot]

        def norm():
            hn_ref[0:rows, :] = _rmsnorm(x(), n1_ref[...]).astype(BF16)

        def columns(n):
            cols = slice(n * MXU_COLS, (n + 1) * MXU_COLS)

            def step():
                z_ref[0:rows, cols] = _dot(hn_ref[0:rows, :], win_ref[:, cols])
            return step

        def pool():
            u_ref[HIST:HIST + rows, :] = z_ref[0:rows, 0:D_POOL]
            for gi, w in enumerate(POOL_WINDOWS):
                lanes = slice(gi * POOL_GROUP, (gi + 1) * POOL_GROUP)
                acc = u_ref[HIST:HIST + rows, lanes]
                u = acc
                for j in range(1, w):
                    acc = acc + u_ref[HIST - j:HIST - j + rows, lanes]
                m = acc * _pool_count(start, offset, rows, w) - u
                y = _dot(m.astype(BF16), wpool_ref[gi]) * pscale_ref[:, lanes]
                mix_ref[0:rows, lanes] = y.astype(BF16)
            u_ref[0:HIST, :] = u_ref[rows:rows + HIST, :]

        n_pool = D_POOL // MXU_COLS
        tiles = [columns(n) for n in range(D_IN // MXU_COLS)]
        return [norm] + tiles[:n_pool] + [pool] + tiles[n_pool:]

    def lockstep(routines, tick):
        live = list(routines)
        while live:
            for routine in list(live):
                try:
                    next(routine)
                except StopIteration:
                    live.remove(routine)
            tick()

    def attend(x, slot, rows, chunk, state_ref, store, filler=()):
        z_ref, mix_ref = z_refs[slot], mix_refs[slot]
        n_blk, n_chunks = chunk // SUBLANES, rows // chunk
        part = lambda n: slice(D_POOL + n * D_REC, D_POOL + (n + 1) * D_REC)
        masks = _gla_masks(n_blk)
        intra = {}
        filler = list(filler)
        rounds = [0]

        def tick(every):
            rounds[0] += 1
            if filler and rounds[0] % every == 0:
                filler.pop(0)()

        def within(c):
            span = slice(c * chunk, (c + 1) * chunk)
            per_head = yield from _gla_intra(z_ref[span, part(0)], z_ref[span, part(1)], lb, z_ref[span, part(2)],
                                             n_blk, masks)
            for h in range(N_HEADS):
                intra[h, c] = per_head[h]

        def carry(h):
            st = state_ref[h]
            for c in range(n_chunks):
                span = slice(c * chunk, (c + 1) * chunk)
                o, st = _gla_state(intra.pop((h, c)), z_ref[span, _rec_col(h, 2)], st)
                yield
                o = o * lax.rsqrt(jnp.mean(o * o, axis=-1, keepdims=True) + EPS) * gnorm
                o = o * _silu(z_ref[span, _rec_col(h, 3)])
                mix_ref[span, D_POOL + h * HEAD:D_POOL + (h + 1) * HEAD] = o.astype(BF16)
                yield
            state_ref[h] = st

        acc = _dot(mix_ref[0:rows, 0:D_POOL], wout_ref[0:D_POOL, :])
        lockstep([within(c) for c in range(n_chunks)], lambda: tick(INTRA_ROUNDS_PER_FILLER))
        lockstep([carry(h) for h in range(N_HEADS)], lambda: tick(1))
        for lo in range(D_POOL, D_MODEL, MXU_COLS):
            acc = acc + _dot(mix_ref[0:rows, lo:lo + MXU_COLS], wout_ref[lo:lo + MXU_COLS, :])
            if filler:
                filler.pop(0)()
        while filler:
            filler.pop(0)()
        store(x() + acc)

    def emit(steps):
        for step in steps:
            step()

    def store_meta(y):
        xmo_ref[0] = y

    @pl.when((b == 0) & (t == 0))
    def _():
        u_ref[0:HIST, :] = jnp.zeros((HIST, D_POOL), F32)
        s0_ref[...] = jnp.zeros(s0_ref.shape, F32)
        xm = lambda: xm_ref[0]
        emit(project(xm, 0, N_META, 0, 0))
        pool0_ref[...] = u_ref[0:HIST, :]
        attend(xm, 0, N_META, N_META, s0_ref, store_meta)
        emit(project(lambda: x_ref[0, 0:sub, :], 0, sub, N_META, 0))

    @pl.when(t == 0)
    def _():
        st_ref[...] = s0_ref[...]

    def trip(m, carry):
        for kk in range(unroll):
            i = m * unroll + kk
            r0 = pl.multiple_of(i * sub, sub)
            poolo_ref[0] = u_ref[HIST - POOL_BUF:HIST, :]
            if kk == unroll - 1:
                tile_end = i == n_sub - 1
                seq_end = tile_end & (t == n_tiles - 1)
                u_ref[0:HIST, :] = jnp.where(seq_end, pool0_ref[...], u_ref[0:HIST, :])
                rn = pl.multiple_of(jnp.minimum(i + 1, n_sub - 1) * sub, sub)

                def x_next(tile_end=tile_end, rn=rn):
                    return jnp.where(tile_end, xn_ref[0], x_ref[0, pl.ds(rn, sub), :])
            else:
                def x_next(r0=r0):
                    return x_ref[0, pl.ds(r0 + sub, sub), :]

            def x(r0=r0):
                return x_ref[0, pl.ds(r0, sub), :]

            def store(y, r0=r0):
                xo_ref[0, pl.ds(r0, sub), :] = y

            attend(x, kk % 2, sub, chunk, st_ref, store, filler=project(x_next, (kk + 1) % 2, sub, N_META, 0))
        return carry

    lax.fori_loop(0, n_sub // unroll, trip, 0)

    for zero_ref in zero_refs:
        zero_ref[...] = jnp.zeros(zero_ref.shape, zero_ref.dtype)

    @pl.when(t == n_tiles - 1)
    def _():
        for h in range(N_HEADS):
            so_ref[0, h] = st_ref[h].T


def _prompt_mixer(x, xm, p, layer, acc, *, tile, sub, chunk, name, zero_shapes=()):
    n_seq, t_len, _ = x.shape
    n_tiles, n_sub = t_len // tile, tile // sub
    assert t_len % tile == 0 and n_sub % 2 == 0 and sub % chunk == 0
    alias_specs, alias_args, n_alias = _stacked(acc)
    n_in = 10

    def next_rows(b, t):
        nxt = jnp.minimum(b * n_tiles + t + 1, n_seq * n_tiles - 1)
        return nxt // n_tiles, (nxt % n_tiles) * n_sub, 0

    per_layer = n_seq * n_tiles // DEPTH
    assert n_seq * n_tiles == per_layer * DEPTH
    zero_specs = []
    for shape, axis in zero_shapes:
        assert shape[0] == DEPTH and shape[axis] % per_layer == 0
        block = list(shape)
        block[0], block[axis] = 1, shape[axis] // per_layer

        def index(b, t, axis=axis, rank=len(shape)):
            step = b * n_tiles + t
            idx = [0] * rank
            idx[0], idx[axis] = step // per_layer, step % per_layer
            return tuple(idx)

        zero_specs.append(pl.BlockSpec(tuple(block), index))
    kern = functools.partial(_prompt_mixer_kernel, n_alias=n_alias, n_zero=len(zero_shapes), layer=layer,
                             tile=tile, sub=sub, chunk=chunk)
    return pl.pallas_call(
        kern,
        grid=(n_seq, n_tiles),
        in_specs=[
            pl.BlockSpec((1, tile, D_MODEL), lambda b, t: (b, t, 0)),
            pl.BlockSpec((1, sub, D_MODEL), next_rows),
            _const_spec((1, N_META, D_MODEL)),
            _layer_spec((1, D_MODEL), layer),
            _layer_spec((D_MODEL, D_IN), layer),
            _layer_spec((len(POOL_WINDOWS), POOL_GROUP, POOL_GROUP), layer),
            _layer_spec((1, D_POOL), layer),
            _const_spec((DEPTH, D_REC)),
            _layer_spec((1, HEAD), layer),
            _layer_spec((D_MODEL, D_MODEL), layer),
        ] + alias_specs,
        out_specs=[
            pl.BlockSpec((1, tile, D_MODEL), lambda b, t: (b, t, 0)),
            pl.BlockSpec((1, N_META, D_MODEL), lambda b, t: (0, 0, 0)),
            pl.BlockSpec((None, 1, POOL_BUF, D_POOL), lambda b, t: (layer, b, 0, 0)),
            pl.BlockSpec((None, 1, N_HEADS, HEAD, HEAD), lambda b, t: (layer, b, 0, 0, 0)),
        ] + zero_specs,
        out_shape=[
            jax.ShapeDtypeStruct((n_seq, t_len, D_MODEL), F32),
            jax.ShapeDtypeStruct((1, N_META, D_MODEL), F32),
            jax.ShapeDtypeStruct((DEPTH, n_seq, POOL_BUF, D_POOL), F32),
            jax.ShapeDtypeStruct((DEPTH, n_seq, N_HEADS, HEAD, HEAD), F32),
        ] + [jax.ShapeDtypeStruct(shape, F32) for shape, _ in zero_shapes],
        scratch_shapes=[
            pltpu.VMEM((sub, D_IN), F32),
            pltpu.VMEM((sub, D_IN), F32),
            pltpu.VMEM((sub, D_MODEL), BF16),
            pltpu.VMEM((sub, D_MODEL), BF16),
            pltpu.VMEM((HIST + sub, D_POOL), F32),
            pltpu.VMEM((N_HEADS, HEAD, HEAD), F32),
            pltpu.VMEM((sub, D_MODEL), BF16),
            pltpu.VMEM((HIST, D_POOL), F32),
            pltpu.VMEM((N_HEADS, HEAD, HEAD), F32),
        ],
        input_output_aliases={n_in + k: 2 + k for k in range(n_alias)},
        compiler_params=pltpu.CompilerParams(
            dimension_semantics=("arbitrary", "arbitrary"),
            vmem_limit_bytes=BIG_VMEM_LIMIT if zero_shapes else VMEM_LIMIT),
        name=name,
    )(x, x, xm, p["n1"], p["w_in"], p["w_pool"], p["pool_scale"], p["lb_raw"], p["g_norm"], p["w_out"],
      *alias_args)


def _prompt_ffn_kernel(*refs, n_alias, tile, final):
    x_ref, xm_ref, n2_ref, wa_ref, wb_ref, cw_ref, cb_ref, wd_ref, fg_ref = refs[:9]
    xo_ref, xmo_ref, convo_ref, a_ref, conv0_ref = refs[9 + n_alias:]
    b, t = pl.program_id(0), pl.program_id(1)
    lo = CONV_HIST - (CONV_W - 1)

    def ffn(x, rows):
        hn = _rmsnorm(x, n2_ref[...]).astype(BF16)
        a_ref[CONV_HIST:CONV_HIST + rows, :] = _dot(hn, wa_ref[...])
        gate = _dot(hn, wb_ref[...])
        ac = cb_ref[...]
        for j in range(CONV_W):
            ac = ac + cw_ref[j:j + 1, :] * a_ref[lo + j:lo + j + rows, :]
        act = (_gelu(ac) * gate).astype(BF16)
        return x + _dot(act, wd_ref[...])

    @pl.when((b == 0) & (t == 0))
    def _():
        a_ref[lo:CONV_HIST, :] = jnp.zeros((CONV_W - 1, D_FF), F32)
        xmo_ref[0] = ffn(xm_ref[0], N_META)
        conv0_ref[...] = a_ref[lo + N_META:CONV_HIST + N_META, :]

    @pl.when(t == 0)
    def _():
        a_ref[lo:CONV_HIST, :] = conv0_ref[...]

    xo = ffn(x_ref[0], tile)
    if final:
        xo = _rmsnorm(xo, fg_ref[...])
    xo_ref[0] = xo

    @pl.when(t == pl.num_programs(1) - 1)
    def _():
        convo_ref[0] = a_ref[lo + tile:CONV_HIST + tile, :]

    a_ref[lo:CONV_HIST, :] = a_ref[lo + tile:CONV_HIST + tile, :]


def _prompt_ffn(x, xm, p, layer, weights, acc, *, tile, final, name):
    n_seq, t_len, _ = x.shape
    w_a, w_b, w_down = weights
    alias_specs, alias_args, n_alias = _stacked(acc)
    n_in = 9
    kern = functools.partial(_prompt_ffn_kernel, n_alias=n_alias, tile=tile, final=final)
    return pl.pallas_call(
        kern,
        grid=(n_seq, t_len // tile),
        in_specs=[
            pl.BlockSpec((1, tile, D_MODEL), lambda b, t: (b, t, 0)),
            _const_spec((1, N_META, D_MODEL)),
            _layer_spec((1, D_MODEL), layer),
            _const_spec((D_MODEL, D_FF)),
            _const_spec((D_MODEL, D_FF)),
            _layer_spec((CONV_W, D_FF), layer),
            _layer_spec((1, D_FF), layer),
            _const_spec((D_FF, D_MODEL)),
            _const_spec((1, D_MODEL)),
        ] + alias_specs,
        out_specs=[
            pl.BlockSpec((1, tile, D_MODEL), lambda b, t: (b, t, 0)),
            pl.BlockSpec((1, N_META, D_MODEL), lambda b, t: (0, 0, 0)),
            pl.BlockSpec((None, 1, CONV_W - 1, D_FF), lambda b, t: (layer, b, 0, 0)),
        ],
        out_shape=[
            jax.ShapeDtypeStruct((n_seq, t_len, D_MODEL), F32),
            jax.ShapeDtypeStruct((1, N_META, D_MODEL), F32),
            jax.ShapeDtypeStruct((DEPTH, n_seq, CONV_W - 1, D_FF), F32),
        ],
        scratch_shapes=[
            pltpu.VMEM((CONV_HIST + tile, D_FF), F32),
            pltpu.VMEM((CONV_W - 1, D_FF), F32),
        ],
        input_output_aliases={n_in + k: 2 + k for k in range(n_alias)},
        compiler_params=pltpu.CompilerParams(
            dimension_semantics=("arbitrary", "arbitrary"), vmem_limit_bytes=VMEM_LIMIT),
        name=name,
    )(x, xm, p["n2"], w_a, w_b, p["conv_w"], p["conv_b"], w_down, p["final_g"], *alias_args)


S_ROWS = DEC_BATCH * DEC_SEQ
S_SLOT = SUBLANES * DEC_BATCH
N_SRC = DEC_SEQ + 3
SEQ_BLOCK = 16


def _sample_mixer_kernel(*refs, n_alias, layer):
    (x_ref, n1_ref, win_ref, wpool_ref, pscale_ref, lbraw_ref, gnorm_ref, wout_ref, sp_ref, s_ref) = refs[:10]
    (xo_ref, spo_ref, so_ref, z_ref, mix_ref, oin_ref, src_ref, gather_ref, scatter_ref,
     qd8_ref, kd8_ref, v8_ref, o8_ref) = refs[10 + n_alias:]
    step = pl.program_id(0)
    tok = lambda t: slice(t * DEC_BATCH, (t + 1) * DEC_BATCH)
    rec = lambda part: [_rec_col(h, part) for h in range(N_HEADS)]

    def heads(rows, part):
        return jnp.concatenate([z_ref[rows, c] for c in rec(part)], axis=1)

    @pl.when(step == 0)
    def _():
        hn = _rmsnorm(x_ref[...], n1_ref[...]).astype(BF16)
        z_ref[...] = _dot(hn, win_ref[...])

        def ext(j, lanes):
            if j >= POOL_BUF:
                return z_ref[tok(j - POOL_BUF), lanes]
            return sp_ref[j, :, lanes]

        for gi, w in enumerate(POOL_WINDOWS):
            lanes = slice(gi * POOL_GROUP, (gi + 1) * POOL_GROUP)
            ms = []
            for t in range(DEC_SEQ):
                acc = ext(POOL_BUF + t, lanes)
                u = acc
                for j in range(1, w):
                    acc = acc + ext(POOL_BUF + t - j, lanes)
                ms.append(acc * (1.0 / w) - u)
            y = _dot(jnp.concatenate(ms, axis=0).astype(BF16), wpool_ref[gi]) * pscale_ref[:, lanes]
            mix_ref[:, lanes] = y
        full = slice(0, D_POOL)
        for j in range(POOL_BUF):
            spo_ref[j] = ext(j + DEC_SEQ, full)

        lb = _lower_bound(lbraw_ref[...], layer)
        qs, ks, gs, vs = [], [], [], []
        for t in range(DEC_SEQ):
            qs.append(_silu(heads(tok(t), 0)) * Q_SCALE)
            log_f, k = _forget_gate(heads(tok(t), 1), lb)
            ks.append(k)
            gs.append(log_f if t == 0 else gs[-1] + log_f)
            vs.append(heads(tok(t), 2))
        g_last = gs[-1]
        for t in range(DEC_SEQ):
            acc = None
            for s in range(t + 1):
                prod = qs[t] * ks[s]
                if s < t:
                    prod = prod * jnp.exp(gs[t] - gs[s])
                parts = []
                for h in range(N_HEADS):
                    hl = slice(h * HEAD, (h + 1) * HEAD)
                    a = jnp.sum(prod[:, hl], axis=-1, keepdims=True)
                    parts.append(a * vs[s][:, hl])
                term = jnp.concatenate(parts, axis=1)
                acc = term if acc is None else acc + term
            oin_ref[tok(t), :] = acc

        row = lax.broadcasted_iota(jnp.int32, (S_SLOT, N_SRC * DEC_BATCH), 0)
        col = lax.broadcasted_iota(jnp.int32, (S_SLOT, N_SRC * DEC_BATCH), 1)
        hit = ((row & (SUBLANES - 1)) == (col >> 7)) & ((row >> 3) == (col & (DEC_BATCH - 1)))
        gather_ref[...] = jnp.where(hit, 1.0, 0.0).astype(BF16)
        row = lax.broadcasted_iota(jnp.int32, (S_ROWS, S_SLOT), 0)
        col = lax.broadcasted_iota(jnp.int32, (S_ROWS, S_SLOT), 1)
        hit = ((col & (SUBLANES - 1)) == (row >> 7)) & ((col >> 3) == (row & (DEC_BATCH - 1)))
        scatter_ref[...] = jnp.where(hit, 1.0, 0.0).astype(BF16)

        for t in range(DEC_SEQ):
            src_ref[tok(t), :] = (ks[t] * jnp.exp(g_last - gs[t])).astype(BF16)
        rest = jnp.exp(g_last)
        for part in range(3):
            piece = rest.astype(BF16)
            src_ref[tok(DEC_SEQ + part), :] = piece
            rest = rest - piece.astype(F32)
        kd8_ref[...] = _dot(gather_ref[...], src_ref[...])
        for t in range(DEC_SEQ):
            src_ref[tok(t), :] = (qs[t] * jnp.exp(gs[t])).astype(BF16)
        qd8_ref[...] = _dot(gather_ref[:, 0:S_ROWS], src_ref[0:S_ROWS, :])
        for t in range(DEC_SEQ):
            src_ref[tok(t), :] = vs[t].astype(BF16)
        v8_ref[...] = _dot(gather_ref[:, 0:S_ROWS], src_ref[0:S_ROWS, :])

    slot = lax.broadcasted_iota(jnp.int32, (SUBLANES, HEAD), 0)
    ones8 = jnp.where((slot >= DEC_SEQ) & (slot < N_SRC), 1.0, 0.0).astype(BF16)
    base = pl.multiple_of(step * (SEQ_BLOCK * SUBLANES), SEQ_BLOCK * SUBLANES)
    for j in range(SEQ_BLOCK):
        rows = pl.ds(base + j * SUBLANES, SUBLANES)
        for h in range(N_HEADS):
            hl = slice(h * HEAD, (h + 1) * HEAD)
            st = s_ref[j, h]
            kd = kd8_ref[rows, hl].astype(BF16)
            o8_ref[rows, hl] = _dot(qd8_ref[rows, hl].astype(BF16), st.astype(BF16))
            so_ref[j, h] = _dot_tn(kd, ones8) * st + _dot_tn(kd, v8_ref[rows, hl].astype(BF16))

    @pl.when(step == pl.num_programs(0) - 1)
    def _():
        o8 = o8_ref[...]
        hi = o8.astype(BF16)
        lo = (o8 - hi.astype(F32)).astype(BF16)
        o = oin_ref[...] + _dot(scatter_ref[...], hi) + _dot(scatter_ref[...], lo)
        for h in range(N_HEADS):
            hl = slice(h * HEAD, (h + 1) * HEAD)
            oh = o[:, hl]
            oh = oh * lax.rsqrt(jnp.mean(oh * oh, axis=-1, keepdims=True) + EPS) * gnorm_ref[...]
            mix_ref[:, D_POOL + h * HEAD:D_POOL + (h + 1) * HEAD] = oh * _silu(z_ref[:, _rec_col(h, 3)])
        xo_ref[...] = x_ref[...] + _dot(mix_ref[...].astype(BF16), wout_ref[...])


def _sample_mixer(x, p, layer, state_pool, state_rec, acc, *, name):
    alias_specs, alias_args, n_alias = _stacked(acc)
    n_in = 10
    kern = functools.partial(_sample_mixer_kernel, n_alias=n_alias, layer=layer)
    rec_block = (None, SEQ_BLOCK, N_HEADS, HEAD, HEAD)
    return pl.pallas_call(
        kern,
        grid=(DEC_BATCH // SEQ_BLOCK,),
        in_specs=[
            _const_spec((S_ROWS, D_MODEL)),
            _layer_spec((1, D_MODEL), layer),
            _layer_spec((D_MODEL, D_IN), layer),
            _layer_spec((len(POOL_WINDOWS), POOL_GROUP, POOL_GROUP), layer),
            _layer_spec((1, D_POOL), layer),
            _const_spec((DEPTH, D_REC)),
            _layer_spec((1, HEAD), layer),
            _layer_spec((D_MODEL, D_MODEL), layer),
            _layer_spec((POOL_BUF, DEC_BATCH, D_POOL), layer),
            pl.BlockSpec(rec_block, lambda s: (layer, s, 0, 0, 0)),
        ] + alias_specs,
        out_specs=[
            pl.BlockSpec((S_ROWS, D_MODEL), lambda s: (0, 0)),
            pl.BlockSpec((None, POOL_BUF, DEC_BATCH, D_POOL), lambda s: (layer, 0, 0, 0)),
            pl.BlockSpec(rec_block, lambda s: (layer, s, 0, 0, 0)),
        ],
        out_shape=[
            jax.ShapeDtypeStruct((S_ROWS, D_MODEL), F32),
            jax.ShapeDtypeStruct((DEPTH, POOL_BUF, DEC_BATCH, D_POOL), F32),
            jax.ShapeDtypeStruct((DEPTH, DEC_BATCH, N_HEADS, HEAD, HEAD), F32),
        ],
        scratch_shapes=[
            pltpu.VMEM((S_ROWS, D_IN), F32),
            pltpu.VMEM((S_ROWS, D_MODEL), F32),
            pltpu.VMEM((S_ROWS, D_REC), F32),
            pltpu.VMEM((N_SRC * DEC_BATCH, D_REC), BF16),
            pltpu.VMEM((S_SLOT, N_SRC * DEC_BATCH), BF16),
            pltpu.VMEM((S_ROWS, S_SLOT), BF16),
            pltpu.VMEM((S_SLOT, D_REC), F32),
            pltpu.VMEM((S_SLOT, D_REC), F32),
            pltpu.VMEM((S_SLOT, D_REC), F32),
            pltpu.VMEM((S_SLOT, D_REC), F32),
        ],
        input_output_aliases={n_in + k: 1 + k for k in range(n_alias)},
        compiler_params=pltpu.CompilerParams(
            dimension_semantics=("arbitrary",), vmem_limit_bytes=BIG_VMEM_LIMIT),
        name=name,
    )(x, p["n1"], p["w_in"], p["w_pool"], p["pool_scale"], p["lb_raw"], p["g_norm"], p["w_out"],
      state_pool, state_rec, *alias_args)


FFN_BLOCK = MXU_COLS


def _sample_ffn_kernel(*refs, n_alias, final):
    x_ref, n2_ref, wa_ref, wb_ref, cw_ref, cb_ref, wd_ref, cs_ref, fg_ref = refs[:9]
    xo_ref, cso_ref, wao_ref, wbo_ref, wdo_ref, hn_ref, acc_ref = refs[9 + n_alias:]
    step = pl.program_id(0)
    tok = lambda t: slice(t * DEC_BATCH, (t + 1) * DEC_BATCH)

    @pl.when(step == 0)
    def _():
        hn_ref[...] = _rmsnorm(x_ref[...], n2_ref[...]).astype(BF16)
        acc_ref[...] = jnp.zeros(acc_ref.shape, F32)

    wa, wb, wd = wa_ref[...].astype(BF16), wb_ref[...].astype(BF16), wd_ref[...].astype(BF16)
    wao_ref[...] = wa
    wbo_ref[...] = wb
    wdo_ref[...] = wd
    hn = hn_ref[...]
    a = _dot(hn, wa)
    gate = _dot(hn, wb)
    ext = [cs_ref[:, j, :] for j in range(CONV_W - 1)] + [a[tok(t)] for t in range(DEC_SEQ)]
    acs = []
    for t in range(DEC_SEQ):
        ac = cb_ref[...]
        for j in range(CONV_W):
            ac = ac + cw_ref[j:j + 1, :] * ext[t + j]
        acs.append(ac)
    act = (_gelu(jnp.concatenate(acs, axis=0)) * gate).astype(BF16)
    acc_ref[...] += _dot(act, wd)
    for j in range(CONV_W - 1):
        cso_ref[:, j, :] = ext[DEC_SEQ + j]

    @pl.when(step == pl.num_programs(0) - 1)
    def _():
        xo = x_ref[...] + acc_ref[...]
        if final:
            xo = _rmsnorm(xo, fg_ref[...])
        xo_ref[...] = xo


def _sample_ffn(x, p, layer, state_conv, acc, *, final, name):
    alias_specs, alias_args, n_alias = _stacked(acc)
    n_in = 9
    kern = functools.partial(_sample_ffn_kernel, n_alias=n_alias, final=final)
    cols = lambda shape: pl.BlockSpec((None,) + shape, lambda j: (layer,) + (0,) * (len(shape) - 1) + (j,))
    return pl.pallas_call(
        kern,
        grid=(D_FF // FFN_BLOCK,),
        in_specs=[
            _const_spec((S_ROWS, D_MODEL)),
            _layer_spec((1, D_MODEL), layer),
            cols((D_MODEL, FFN_BLOCK)),
            cols((D_MODEL, FFN_BLOCK)),
            cols((CONV_W, FFN_BLOCK)),
            cols((1, FFN_BLOCK)),
            pl.BlockSpec((None, FFN_BLOCK, D_MODEL), lambda j: (layer, j, 0)),
            cols((DEC_BATCH, CONV_W - 1, FFN_BLOCK)),
            _const_spec((1, D_MODEL)),
        ] + alias_specs,
        out_specs=[
            pl.BlockSpec((S_ROWS, D_MODEL), lambda j: (0, 0)),
            cols((DEC_BATCH, CONV_W - 1, FFN_BLOCK)),
            pl.BlockSpec((D_MODEL, FFN_BLOCK), lambda j: (0, j)),
            pl.BlockSpec((D_MODEL, FFN_BLOCK), lambda j: (0, j)),
            pl.BlockSpec((FFN_BLOCK, D_MODEL), lambda j: (j, 0)),
        ],
        out_shape=[
            jax.ShapeDtypeStruct((S_ROWS, D_MODEL), F32),
            jax.ShapeDtypeStruct((DEPTH, DEC_BATCH, CONV_W - 1, D_FF), F32),
            jax.ShapeDtypeStruct((D_MODEL, D_FF), BF16),
            jax.ShapeDtypeStruct((D_MODEL, D_FF), BF16),
            jax.ShapeDtypeStruct((D_FF, D_MODEL), BF16),
        ],
        scratch_shapes=[
            pltpu.VMEM((S_ROWS, D_MODEL), BF16),
            pltpu.VMEM((S_ROWS, D_MODEL), F32),
        ],
        input_output_aliases={n_in + k: 1 + k for k in range(n_alias)},
        compiler_params=pltpu.CompilerParams(
            dimension_semantics=("arbitrary",), vmem_limit_bytes=VMEM_LIMIT),
        name=name,
    )(x, p["n2"], p["w_a"], p["w_b"], p["conv_w"], p["conv_b"], p["w_down"], state_conv, p["final_g"],
      *alias_args)


MIXER_TILE = 1024
MIXER_SUB = 256
MIXER_CHUNK = 64
FFN_TILE = 512


def kernel(x_prompt, x_sample, state_pool, state_rec, state_conv, meta_tokens, norm1_g, w_in, w_pool, pool_scale, lb_raw, rec_norm_g, w_out, norm2_g, w_a, w_b, conv_w, conv_b, w_down, final_g):
    p = dict(
        n1=norm1_g.reshape(DEPTH, 1, D_MODEL), w_in=w_in.astype(BF16),
        w_pool=w_pool.astype(BF16), pool_scale=pool_scale.reshape(DEPTH, 1, D_POOL), lb_raw=lb_raw,
        g_norm=rec_norm_g.reshape(DEPTH, 1, HEAD), w_out=w_out.astype(BF16),
        n2=norm2_g.reshape(DEPTH, 1, D_MODEL), w_a=w_a, w_b=w_b,
        conv_w=conv_w, conv_b=conv_b.reshape(DEPTH, 1, D_FF), w_down=w_down,
        final_g=final_g.reshape(1, D_MODEL),
    )
    last = DEPTH - 1
    n_seq = x_prompt.shape[0]
    zeros = lambda *shape: jnp.zeros((DEPTH,) + shape, F32)

    xp, xm = x_prompt, meta_tokens.reshape(1, N_META, D_MODEL)
    xs = x_sample.transpose(1, 0, 2).reshape(S_ROWS, D_MODEL)
    pool_rows = state_pool.transpose(0, 2, 1, 3)
    p_mix = [zeros(n_seq, POOL_BUF, D_POOL), zeros(n_seq, N_HEADS, HEAD, HEAD)]
    p_ffn = [zeros(n_seq, CONV_W - 1, D_FF)]
    sample_results = [((DEPTH, POOL_BUF, DEC_BATCH, D_POOL), 2), ((DEPTH, DEC_BATCH, N_HEADS, HEAD, HEAD), 1),
                      ((DEPTH, DEC_BATCH, CONV_W - 1, D_FF), 1)]
    for l in range(DEPTH):
        xp, xm, *p_mix = _prompt_mixer(xp, xm, p, l, p_mix, tile=MIXER_TILE, sub=MIXER_SUB, chunk=MIXER_CHUNK,
                                       name=f"prompt_mixer_{l}", zero_shapes=sample_results if l == 0 else ())
        if l == 0:
            p_mix, (pool_s, rec_s, conv_s) = p_mix[:2], p_mix[2:]
            s_mix, s_ffn = [pool_s, rec_s], [conv_s]
        xs, *s_mix = _sample_mixer(xs, p, l, pool_rows, state_rec, s_mix, name=f"sample_mixer_{l}")
        xs, conv_s, *ffn_weights = _sample_ffn(xs, p, l, state_conv, s_ffn, final=(l == last),
                                               name=f"sample_ffn_{l}")
        s_ffn = [conv_s]
        xp, xm, *p_ffn = _prompt_ffn(xp, xm, p, l, ffn_weights, p_ffn, tile=FFN_TILE, final=(l == last),
                                     name=f"prompt_ffn_{l}")
    pool_p, rec_p = p_mix
    conv_p, = p_ffn
    pool_s, rec_s = s_mix
    y_sample = xs.reshape(DEC_SEQ, DEC_BATCH, D_MODEL).transpose(1, 0, 2)

    return xp, y_sample, pool_p, rec_p, conv_p, pool_s.transpose(0, 2, 1, 3), rec_s, conv_s
```

```python
import functools

import jax
import jax.numpy as jnp
from jax import lax
from jax.experimental import pallas as pl
from jax.experimental.pallas import tpu as pltpu

F32 = jnp.float32
BF16 = jnp.bfloat16

D_MODEL = 1024
DEPTH = 4
D_POOL = 512
D_REC = 512
N_HEADS = 4
HEAD = 128
N_PARTS = 4
D_IN = D_POOL + N_PARTS * D_REC
D_FF = 2816
POOL_WINDOWS = (2, 4, 8, 16)
POOL_GROUP = 128
POOL_BUF = 15
N_META = 16
DEC_BATCH = 128
DEC_SEQ = 4
CONV_W = 3
EPS = 1e-6
Q_SCALE = HEAD ** -0.5
SQRT_HALF = 0.7071067811865476

SUBLANES = 8
BF16_ROWS = 16
MXU_COLS = 256
LOCKSTEP_CHUNKS = 2
INTRA_ROUNDS_PER_FILLER = 20
HIST = 16
CONV_HIST = 8
VMEM_LIMIT = 56 * 1024 * 1024
BIG_VMEM_LIMIT = 62 * 1024 * 1024


def _rmsnorm(x, g):
    return x * lax.rsqrt(jnp.mean(x * x, axis=-1, keepdims=True) + EPS) * g


def _silu(x):
    return x * (1.0 / (1.0 + jnp.exp(-x)))


def _gelu(x):
    return 0.5 * x * (1.0 + lax.erf(x * SQRT_HALF))


def _lower_bound(lb_raw, layer):
    m = jnp.max(lb_raw, axis=0, keepdims=True)
    e = jnp.exp(lb_raw - m)
    p = e / jnp.sum(e, axis=0, keepdims=True)
    lb = jnp.zeros_like(m)
    for i in range(1, layer + 1):
        lb = lb + p[i:i + 1]
    return lb


def _forget_gate(f, lb):
    m = jnp.exp(-jnp.abs(f))
    r = 1.0 / (1.0 + m)
    sig_neg = jnp.where(f >= 0, m * r, r)
    log_sig = jnp.minimum(f, 0.0) - jnp.log(1.0 + m)
    a = jnp.log(lb)
    b = jnp.log(1.0 - lb) + log_sig
    log_f = jnp.maximum(a, b) + jnp.log(1.0 + jnp.exp(-jnp.abs(a - b)))
    return log_f, (1.0 - lb) * sig_neg


def _dot(a, b):
    return jnp.dot(a, b, preferred_element_type=F32)


def _dot_nt(a, b):
    return lax.dot_general(a, b, (((1,), (1,)), ((), ())), preferred_element_type=F32)


def _dot_tn(a, b):
    return lax.dot_general(a, b, (((0,), (0,)), ((), ())), preferred_element_type=F32)


HALF = SUBLANES // 2


def _gla_masks(n_blk):
    c_len = SUBLANES * n_blk
    lane = lax.broadcasted_iota(jnp.int32, (SUBLANES, c_len), 1)
    row = lax.broadcasted_iota(jnp.int32, (SUBLANES, c_len), 0)
    diag = lane - row
    near = [[diag == SUBLANES * i - r for r in range(HALF)] for i in range(n_blk)]
    band = [(diag <= SUBLANES * i) & (diag > SUBLANES * i - HALF) for i in range(n_blk)]
    across = [(row >= HALF) & (lane >= SUBLANES * i) & (lane < SUBLANES * i + HALF) for i in range(n_blk)]
    below = {c: lane < SUBLANES * c for c in range(1, n_blk)}
    return near, band, across, below


def _gla_intra(q_raw, f_raw, lb, v, n_blk, masks):
    near, band, across, below = masks
    c_len = SUBLANES * n_blk
    width = q_raw.shape[1]
    heads = [slice(lo, lo + HEAD) for lo in range(0, width, HEAD)]
    sub = lax.broadcasted_iota(jnp.int32, (SUBLANES, width), 0)
    blk = lambda a, i: a[SUBLANES * i:SUBLANES * (i + 1)]
    square = lambda x: _dot_nt(x, x)
    rows_bf16 = lambda tiles: jnp.concatenate(tiles, axis=0).astype(BF16)

    tot, q_in, k_out, halves, near_rows = [], [], [], [], []
    for i in range(n_blk):
        qi = _silu(blk(q_raw, i)) * Q_SCALE
        gi, ki = _forget_gate(blk(f_raw, i), lb)
        yield
        local = gi
        for s in (1, 2, 4):
            local = local + jnp.where(sub >= s, pltpu.roll(local, s, 0), 0.0)
        yield
        last = local[SUBLANES - 1:SUBLANES]
        tot.append(jnp.exp(last))
        q_in.append(qi * jnp.exp(local))
        k_out.append(ki * jnp.exp(last - local))
        mid = jnp.broadcast_to(local[HALF - 1:HALF], (SUBLANES, width))
        halves.append(jnp.where(sub >= HALF, qi, ki) * jnp.exp(-jnp.abs(local - mid)))
        yield
        gate = jnp.exp(gi)
        w, rows = qi, [jnp.zeros((SUBLANES, c_len), F32) for _ in heads]
        for r in range(HALF):
            if r > 0:
                w = w * (gate if r == 1 else pltpu.roll(gate, r - 1, 0))
            pair = w * (ki if r == 0 else pltpu.roll(ki, r, 0))
            for n, hl in enumerate(heads):
                rows[n] = jnp.where(near[i][r], jnp.sum(pair[:, hl], axis=-1, keepdims=True), rows[n])
            yield
        near_rows.append(rows)

    spans = {}

    def span(a, b):
        if a >= b:
            return None
        if (a, b) not in spans:
            if (a + 1, b) in spans:
                spans[a, b] = tot[a] * spans[a + 1, b]
            else:
                head = span(a, b - 1)
                spans[a, b] = tot[b - 1] if head is None else head * tot[b - 1]
        return spans[a, b]

    scaled = lambda x, s: x if s is None else x * s
    q_dec = rows_bf16([scaled(q_in[i], span(0, i)) for i in range(n_blk)])
    k_dec = rows_bf16([scaled(k_out[i], span(i + 1, n_blk)) for i in reversed(range(n_blk))][::-1])
    yield

    level_prod, level_row = {}, {}
    half = n_blk // 2
    while half >= 1:
        tiles, upper = {}, []
        for cut in range(half, n_blk, 2 * half):
            for i in range(cut, cut + half):
                tiles[i] = scaled(q_in[i], span(cut, i))
                upper.append(i)
            for i in reversed(range(cut - half, cut)):
                tiles[i] = scaled(k_out[i], span(i + 1, cut))
        both = rows_bf16([tiles[i] for i in range(n_blk)])
        if (len(upper) * SUBLANES) % BF16_ROWS:
            upper = list(range(n_blk))
        left = both if len(upper) == n_blk else rows_bf16([tiles[i] for i in upper])
        level_prod[half] = [_dot_nt(left[:, hl], both[:, hl]) for hl in heads]
        level_row[half] = {i: n for n, i in enumerate(upper)}
        half //= 2
        yield
    halves = rows_bf16(halves)
    half_prod = [square(halves[:, hl]) for hl in heads]
    yield

    v_b, decay, out = v.astype(BF16), span(0, n_blk), []
    for n, hl in enumerate(heads):
        rows = []
        for i in range(n_blk):
            row = jnp.where(band[i], near_rows[i][n], jnp.where(across[i], blk(half_prod[n], i), 0.0))
            half = 1
            while half < n_blk:
                if i % (2 * half) >= half:
                    prod = blk(level_prod[half][n], level_row[half][i])
                    row = jnp.where(below[(i // (2 * half)) * 2 * half + half], prod, row)
                half *= 2
            rows.append(row)
        out.append((_dot(rows_bf16(rows), v_b[:, hl]), q_dec[:, hl], k_dec[:, hl], decay[:, hl]))
        yield
    return out


def _gla_state(intra, v, st):
    o_intra, q_dec, k_dec, decay = intra
    out = o_intra + _dot_nt(q_dec, st.astype(BF16))
    return out, st * decay + _dot_tn(v.astype(BF16), k_dec)


def _pool_count(start, offset, rows, w):
    if start >= max(POOL_WINDOWS) - 1:
        return 1.0 / w
    pos = start + offset + lax.broadcasted_iota(jnp.int32, (rows, 1), 0)
    return 1.0 / jnp.minimum(pos + 1, w).astype(F32)


def _rec_col(h, part):
    lo = D_POOL + part * D_REC + h * HEAD
    return slice(lo, lo + HEAD)


def _layer_spec(shape, layer):
    zeros = (0,) * len(shape)
    return pl.BlockSpec((None,) + tuple(shape), lambda *_: (layer,) + zeros, pipeline_mode=pl.Buffered(1))


def _const_spec(shape):
    zeros = (0,) * len(shape)
    return pl.BlockSpec(tuple(shape), lambda *_: zeros, pipeline_mode=pl.Buffered(1))


def _stacked(acc):
    return [pl.BlockSpec(memory_space=pl.ANY)] * len(acc), list(acc), len(acc)


def _prompt_mixer_kernel(*refs, n_alias, n_zero, layer, tile, sub, chunk):
    (x_ref, xn_ref, xm_ref, n1_ref, win_ref, wpool_ref, pscale_ref, lbraw_ref, gnorm_ref, wout_ref) = refs[:10]
    xo_ref, xmo_ref, poolo_ref, so_ref = refs[10 + n_alias:14 + n_alias]
    zero_refs = refs[14 + n_alias:14 + n_alias + n_zero]
    (z0_ref, z1_ref, mix0_ref, mix1_ref, u_ref, st_ref, hn_ref, pool0_ref, s0_ref) = refs[14 + n_alias + n_zero:]
    z_refs, mix_refs = (z0_ref, z1_ref), (mix0_ref, mix1_ref)
    b, t, n_tiles = pl.program_id(0), pl.program_id(1), pl.num_programs(1)
    n_sub = tile // sub
    lb = _lower_bound(lbraw_ref[...], layer)
    gnorm = gnorm_ref[...]

    def project(x, slot, rows, start, offset):
        z_ref, mix_ref = z_refs[slot], mix_refs[slot]

        def norm():
            hn_ref[0:rows, :] = _rmsnorm(x(), n1_ref[...]).astype(BF16)

        def columns(n):
            cols = slice(n * MXU_COLS, (n + 1) * MXU_COLS)

            def step():
                z_ref[0:rows, cols] = _dot(hn_ref[0:rows, :], win_ref[:, cols])
            return step

        def pool():
            u_ref[HIST:HIST + rows, :] = z_ref[0:rows, 0:D_POOL]
            for gi, w in enumerate(POOL_WINDOWS):
                lanes = slice(gi * POOL_GROUP, (gi + 1) * POOL_GROUP)
                acc = u_ref[HIST:HIST + rows, lanes]
                u = acc
                for j in range(1, w):
                    acc = acc + u_ref[HIST - j:HIST - j + rows, lanes]
                m = acc * _pool_count(start, offset, rows, w) - u
                y = _dot(m.astype(BF16), wpool_ref[gi]) * pscale_ref[:, lanes]
                mix_ref[0:rows, lanes] = y.astype(BF16)
            u_ref[0:HIST, :] = u_ref[rows:rows + HIST, :]

        n_pool = D_POOL // MXU_COLS
        tiles = [columns(n) for n in range(D_IN // MXU_COLS)]
        return [norm] + tiles[:n_pool] + [pool] + tiles[n_pool:]

    def lockstep(routines, tick, done=lambda: False):
        live = list(routines)
        while live and not done():
            for routine in list(live):
                try:
                    next(routine)
                except StopIteration:
                    live.remove(routine)
            tick()
        return live

    def attend(x, slot, rows, chunk, state_ref, store, filler=()):
        z_ref, mix_ref = z_refs[slot], mix_refs[slot]
        n_blk, n_chunks = chunk // SUBLANES, rows // chunk
        part = lambda n: slice(D_POOL + n * D_REC, D_POOL + (n + 1) * D_REC)
        masks = _gla_masks(n_blk)
        intra, ready = {}, set()
        filler = list(filler)
        rounds = [0]

        def tick(every):
            rounds[0] += 1
            if filler and rounds[0] % every == 0:
                filler.pop(0)()

        def within(c):
            span = slice(c * chunk, (c + 1) * chunk)
            per_head = yield from _gla_intra(z_ref[span, part(0)], z_ref[span, part(1)], lb, z_ref[span, part(2)],
                                             n_blk, masks)
            for h in range(N_HEADS):
                intra[h, c] = per_head[h]
            ready.add(c)

        def carry(h):
            for c in range(n_chunks):
                while (h, c) not in intra:
                    yield
                if c == 0:
                    st = state_ref[h]
                span = slice(c * chunk, (c + 1) * chunk)
                o, st = _gla_state(intra.pop((h, c)), z_ref[span, _rec_col(h, 2)], st)
                yield
                o = o * lax.rsqrt(jnp.mean(o * o, axis=-1, keepdims=True) + EPS) * gnorm
                o = o * _silu(z_ref[span, _rec_col(h, 3)])
                mix_ref[span, D_POOL + h * HEAD:D_POOL + (h + 1) * HEAD] = o.astype(BF16)
                yield
            state_ref[h] = st

        acc = _dot(mix_ref[0:rows, 0:D_POOL], wout_ref[0:D_POOL, :])
        running = [carry(h) for h in range(N_HEADS)]
        for lo in range(0, n_chunks, LOCKSTEP_CHUNKS):
            group = range(lo, min(lo + LOCKSTEP_CHUNKS, n_chunks))
            running = lockstep([within(c) for c in group] + running, lambda: tick(INTRA_ROUNDS_PER_FILLER),
                               done=lambda: ready.issuperset(group))
        lockstep(running, lambda: tick(1))
        for lo in range(D_POOL, D_MODEL, MXU_COLS):
            acc = acc + _dot(mix_ref[0:rows, lo:lo + MXU_COLS], wout_ref[lo:lo + MXU_COLS, :])
            if filler:
                filler.pop(0)()
        while filler:
            filler.pop(0)()
        store(x() + acc)

    def emit(steps):
        for step in steps:
            step()

    def store_meta(y):
        xmo_ref[0] = y

    @pl.when((b == 0) & (t == 0))
    def _():
        u_ref[0:HIST, :] = jnp.zeros((HIST, D_POOL), F32)
        s0_ref[...] = jnp.zeros(s0_ref.shape, F32)
        xm = lambda: xm_ref[0]
        emit(project(xm, 0, N_META, 0, 0))
        pool0_ref[...] = u_ref[0:HIST, :]
        attend(xm, 0, N_META, N_META, s0_ref, store_meta)
        emit(project(lambda: x_ref[0, 0:sub, :], 0, sub, N_META, 0))

    @pl.when(t == 0)
    def _():
        st_ref[...] = s0_ref[...]

    for i in range(n_sub):
        r0 = i * sub
        if i == n_sub - 1:
            poolo_ref[0] = u_ref[HIST - POOL_BUF:HIST, :]
            u_ref[0:HIST, :] = jnp.where(t == n_tiles - 1, pool0_ref[...], u_ref[0:HIST, :])

            def x_next():
                return xn_ref[0]
        else:
            def x_next(r0=r0):
                return x_ref[0, r0 + sub:r0 + 2 * sub, :]

        def x(r0=r0):
            return x_ref[0, r0:r0 + sub, :]

        def store(y, r0=r0):
            xo_ref[0, r0:r0 + sub, :] = y

        attend(x, i % 2, sub, chunk, st_ref, store, filler=project(x_next, (i + 1) % 2, sub, N_META, 0))

    for zero_ref in zero_refs:
        zero_ref[...] = jnp.zeros(zero_ref.shape, zero_ref.dtype)

    @pl.when(t == n_tiles - 1)
    def _():
        for h in range(N_HEADS):
            so_ref[0, h] = st_ref[h].T


def _prompt_mixer(x, xm, p, layer, acc, *, tile, sub, chunk, name, zero_shapes=()):
    n_seq, t_len, _ = x.shape
    n_tiles, n_sub = t_len // tile, tile // sub
    assert t_len % tile == 0 and n_sub % 2 == 0 and sub % chunk == 0
    alias_specs, alias_args, n_alias = _stacked(acc)
    n_in = 10

    def next_rows(b, t):
        nxt = jnp.minimum(b * n_tiles + t + 1, n_seq * n_tiles - 1)
        return nxt // n_tiles, (nxt % n_tiles) * n_sub, 0

    per_layer = n_seq * n_tiles // DEPTH
    assert n_seq * n_tiles == per_layer * DEPTH
    zero_specs = []
    for shape, axis in zero_shapes:
        assert shape[0] == DEPTH and shape[axis] % per_layer == 0
        block = list(shape)
        block[0], block[axis] = 1, shape[axis] // per_layer

        def index(b, t, axis=axis, rank=len(shape)):
            step = b * n_tiles + t
            idx = [0] * rank
            idx[0], idx[axis] = step // per_layer, step % per_layer
            return tuple(idx)

        zero_specs.append(pl.BlockSpec(tuple(block), index))
    kern = functools.partial(_prompt_mixer_kernel, n_alias=n_alias, n_zero=len(zero_shapes), layer=layer,
                             tile=tile, sub=sub, chunk=chunk)
    return pl.pallas_call(
        kern,
        grid=(n_seq, n_tiles),
        in_specs=[
            pl.BlockSpec((1, tile, D_MODEL), lambda b, t: (b, t, 0)),
            pl.BlockSpec((1, sub, D_MODEL), next_rows),
            _const_spec((1, N_META, D_MODEL)),
            _layer_spec((1, D_MODEL), layer),
            _const_spec((D_MODEL, D_IN)),
            _layer_spec((len(POOL_WINDOWS), POOL_GROUP, POOL_GROUP), layer),
            _layer_spec((1, D_POOL), layer),
            _const_spec((DEPTH, D_REC)),
            _layer_spec((1, HEAD), layer),
            _const_spec((D_MODEL, D_MODEL)),
        ] + alias_specs,
        out_specs=[
            pl.BlockSpec((1, tile, D_MODEL), lambda b, t: (b, t, 0)),
            pl.BlockSpec((1, N_META, D_MODEL), lambda b, t: (0, 0, 0)),
            pl.BlockSpec((None, 1, POOL_BUF, D_POOL), lambda b, t: (layer, b, 0, 0)),
            pl.BlockSpec((None, 1, N_HEADS, HEAD, HEAD), lambda b, t: (layer, b, 0, 0, 0)),
        ] + zero_specs,
        out_shape=[
            jax.ShapeDtypeStruct((n_seq, t_len, D_MODEL), F32),
            jax.ShapeDtypeStruct((1, N_META, D_MODEL), F32),
            jax.ShapeDtypeStruct((DEPTH, n_seq, POOL_BUF, D_POOL), F32),
            jax.ShapeDtypeStruct((DEPTH, n_seq, N_HEADS, HEAD, HEAD), F32),
        ] + [jax.ShapeDtypeStruct(shape, F32) for shape, _ in zero_shapes],
        scratch_shapes=[
            pltpu.VMEM((sub, D_IN), F32),
            pltpu.VMEM((sub, D_IN), F32),
            pltpu.VMEM((sub, D_MODEL), BF16),
            pltpu.VMEM((sub, D_MODEL), BF16),
            pltpu.VMEM((HIST + sub, D_POOL), F32),
            pltpu.VMEM((N_HEADS, HEAD, HEAD), F32),
            pltpu.VMEM((sub, D_MODEL), BF16),
            pltpu.VMEM((HIST, D_POOL), F32),
            pltpu.VMEM((N_HEADS, HEAD, HEAD), F32),
        ],
        input_output_aliases={n_in + k: 2 + k for k in range(n_alias)},
        compiler_params=pltpu.CompilerParams(
            dimension_semantics=("arbitrary", "arbitrary"),
            vmem_limit_bytes=BIG_VMEM_LIMIT if zero_shapes else VMEM_LIMIT),
        name=name,
    )(x, x, xm, p["n1"], p["w_in"][layer], p["w_pool"], p["pool_scale"], p["lb_raw"], p["g_norm"], p["w_out"][layer],
      *alias_args)


def _prompt_ffn_kernel(*refs, n_alias, n_round, tile, final):
    x_ref, xm_ref, n2_ref, wa_ref, wb_ref, cw_ref, cb_ref, wd_ref, fg_ref = refs[:9]
    f32_refs = refs[9:9 + n_round]
    xo_ref, xmo_ref, convo_ref = refs[9 + n_round + n_alias:12 + n_round + n_alias]
    bf16_refs = refs[12 + n_round + n_alias:12 + 2 * n_round + n_alias]
    a_ref, conv0_ref = refs[12 + 2 * n_round + n_alias:]
    b, t = pl.program_id(0), pl.program_id(1)
    for f32_ref, bf16_ref in zip(f32_refs, bf16_refs):
        bf16_ref[...] = f32_ref[...].astype(BF16)
    lo = CONV_HIST - (CONV_W - 1)

    def ffn(x, rows):
        hn = _rmsnorm(x, n2_ref[...]).astype(BF16)
        a_ref[CONV_HIST:CONV_HIST + rows, :] = _dot(hn, wa_ref[...])
        gate = _dot(hn, wb_ref[...])
        ac = cb_ref[...]
        for j in range(CONV_W):
            ac = ac + cw_ref[j:j + 1, :] * a_ref[lo + j:lo + j + rows, :]
        act = (_gelu(ac) * gate).astype(BF16)
        return x + _dot(act, wd_ref[...])

    @pl.when((b == 0) & (t == 0))
    def _():
        a_ref[lo:CONV_HIST, :] = jnp.zeros((CONV_W - 1, D_FF), F32)
        xmo_ref[0] = ffn(xm_ref[0], N_META)
        conv0_ref[...] = a_ref[lo + N_META:CONV_HIST + N_META, :]

    @pl.when(t == 0)
    def _():
        a_ref[lo:CONV_HIST, :] = conv0_ref[...]

    xo = ffn(x_ref[0], tile)
    if final:
        xo = _rmsnorm(xo, fg_ref[...])
    xo_ref[0] = xo

    @pl.when(t == pl.num_programs(1) - 1)
    def _():
        convo_ref[0] = a_ref[lo + tile:CONV_HIST + tile, :]

    a_ref[lo:CONV_HIST, :] = a_ref[lo + tile:CONV_HIST + tile, :]


def _prompt_ffn(x, xm, p, layer, weights, acc, *, tile, final, name, to_round=()):
    n_seq, t_len, _ = x.shape
    n_tiles = t_len // tile
    w_a, w_b, w_down = weights
    alias_specs, alias_args, n_alias = _stacked(acc)
    n_round = len(to_round)
    n_in = 9 + n_round
    steps = n_seq * n_tiles
    for w in to_round:
        assert w.shape[1] % (steps * BF16_ROWS) == 0
    round_in = [pl.BlockSpec((None, w.shape[1] // steps, w.shape[2]), lambda b, t: (layer + 1, b * n_tiles + t, 0))
                for w in to_round]
    round_out = [pl.BlockSpec((w.shape[1] // steps, w.shape[2]), lambda b, t: (b * n_tiles + t, 0))
                 for w in to_round]
    kern = functools.partial(_prompt_ffn_kernel, n_alias=n_alias, n_round=n_round, tile=tile, final=final)
    return pl.pallas_call(
        kern,
        grid=(n_seq, t_len // tile),
        in_specs=[
            pl.BlockSpec((1, tile, D_MODEL), lambda b, t: (b, t, 0)),
            _const_spec((1, N_META, D_MODEL)),
            _layer_spec((1, D_MODEL), layer),
            _const_spec((D_MODEL, D_FF)),
            _const_spec((D_MODEL, D_FF)),
            _layer_spec((CONV_W, D_FF), layer),
            _layer_spec((1, D_FF), layer),
            _const_spec((D_FF, D_MODEL)),
            _const_spec((1, D_MODEL)),
        ] + round_in + alias_specs,
        out_specs=[
            pl.BlockSpec((1, tile, D_MODEL), lambda b, t: (b, t, 0)),
            pl.BlockSpec((1, N_META, D_MODEL), lambda b, t: (0, 0, 0)),
            pl.BlockSpec((None, 1, CONV_W - 1, D_FF), lambda b, t: (layer, b, 0, 0)),
        ] + round_out,
        out_shape=[
            jax.ShapeDtypeStruct((n_seq, t_len, D_MODEL), F32),
            jax.ShapeDtypeStruct((1, N_META, D_MODEL), F32),
            jax.ShapeDtypeStruct((DEPTH, n_seq, CONV_W - 1, D_FF), F32),
        ] + [jax.ShapeDtypeStruct(w.shape[1:], BF16) for w in to_round],
        scratch_shapes=[
            pltpu.VMEM((CONV_HIST + tile, D_FF), F32),
            pltpu.VMEM((CONV_W - 1, D_FF), F32),
        ],
        input_output_aliases={n_in + k: 2 + k for k in range(n_alias)},
        compiler_params=pltpu.CompilerParams(
            dimension_semantics=("arbitrary", "arbitrary"), vmem_limit_bytes=VMEM_LIMIT),
        name=name,
    )(x, xm, p["n2"], w_a, w_b, p["conv_w"], p["conv_b"], w_down, p["final_g"], *to_round, *alias_args)


S_ROWS = DEC_BATCH * DEC_SEQ
S_SLOT = SUBLANES * DEC_BATCH
N_SRC = DEC_SEQ + 3
SEQ_BLOCK = 16


def _sample_mixer_kernel(*refs, n_alias, layer):
    (x_ref, n1_ref, win_ref, wpool_ref, pscale_ref, lbraw_ref, gnorm_ref, wout_ref, sp_ref, s_ref) = refs[:10]
    (xo_ref, spo_ref, so_ref, z_ref, mix_ref, oin_ref, src_ref, gather_ref, scatter_ref,
     qd8_ref, kd8_ref, v8_ref, o8_ref) = refs[10 + n_alias:]
    step = pl.program_id(0)
    tok = lambda t: slice(t * DEC_BATCH, (t + 1) * DEC_BATCH)
    rec = lambda part: [_rec_col(h, part) for h in range(N_HEADS)]

    def heads(rows, part):
        return jnp.concatenate([z_ref[rows, c] for c in rec(part)], axis=1)

    @pl.when(step == 0)
    def _():
        hn = _rmsnorm(x_ref[...], n1_ref[...]).astype(BF16)
        z_ref[...] = _dot(hn, win_ref[...])

        def ext(j, lanes):
            if j >= POOL_BUF:
                return z_ref[tok(j - POOL_BUF), lanes]
            return sp_ref[j, :, lanes]

        for gi, w in enumerate(POOL_WINDOWS):
            lanes = slice(gi * POOL_GROUP, (gi + 1) * POOL_GROUP)
            ms = []
            for t in range(DEC_SEQ):
                acc = ext(POOL_BUF + t, lanes)
                u = acc
                for j in range(1, w):
                    acc = acc + ext(POOL_BUF + t - j, lanes)
                ms.append(acc * (1.0 / w) - u)
            y = _dot(jnp.concatenate(ms, axis=0).astype(BF16), wpool_ref[gi]) * pscale_ref[:, lanes]
            mix_ref[:, lanes] = y
        full = slice(0, D_POOL)
        for j in range(POOL_BUF):
            spo_ref[j] = ext(j + DEC_SEQ, full)

        lb = _lower_bound(lbraw_ref[...], layer)
        qs, ks, gs, vs = [], [], [], []
        for t in range(DEC_SEQ):
            qs.append(_silu(heads(tok(t), 0)) * Q_SCALE)
            log_f, k = _forget_gate(heads(tok(t), 1), lb)
            ks.append(k)
            gs.append(log_f if t == 0 else gs[-1] + log_f)
            vs.append(heads(tok(t), 2))
        g_last = gs[-1]
        for t in range(DEC_SEQ):
            acc = None
            for s in range(t + 1):
                prod = qs[t] * ks[s]
                if s < t:
                    prod = prod * jnp.exp(gs[t] - gs[s])
                parts = []
                for h in range(N_HEADS):
                    hl = slice(h * HEAD, (h + 1) * HEAD)
                    a = jnp.sum(prod[:, hl], axis=-1, keepdims=True)
                    parts.append(a * vs[s][:, hl])
                term = jnp.concatenate(parts, axis=1)
                acc = term if acc is None else acc + term
            oin_ref[tok(t), :] = acc

        row = lax.broadcasted_iota(jnp.int32, (S_SLOT, N_SRC * DEC_BATCH), 0)
        col = lax.broadcasted_iota(jnp.int32, (S_SLOT, N_SRC * DEC_BATCH), 1)
        hit = ((row & (SUBLANES - 1)) == (col >> 7)) & ((row >> 3) == (col & (DEC_BATCH - 1)))
        gather_ref[...] = jnp.where(hit, 1.0, 0.0).astype(BF16)
        row = lax.broadcasted_iota(jnp.int32, (S_ROWS, S_SLOT), 0)
        col = lax.broadcasted_iota(jnp.int32, (S_ROWS, S_SLOT), 1)
        hit = ((col & (SUBLANES - 1)) == (row >> 7)) & ((col >> 3) == (row & (DEC_BATCH - 1)))
        scatter_ref[...] = jnp.where(hit, 1.0, 0.0).astype(BF16)

        for t in range(DEC_SEQ):
            src_ref[tok(t), :] = (ks[t] * jnp.exp(g_last - gs[t])).astype(BF16)
        rest = jnp.exp(g_last)
        for part in range(3):
            piece = rest.astype(BF16)
            src_ref[tok(DEC_SEQ + part), :] = piece
            rest = rest - piece.astype(F32)
        kd8_ref[...] = _dot(gather_ref[...], src_ref[...])
        for t in range(DEC_SEQ):
            src_ref[tok(t), :] = (qs[t] * jnp.exp(gs[t])).astype(BF16)
        qd8_ref[...] = _dot(gather_ref[:, 0:S_ROWS], src_ref[0:S_ROWS, :])
        for t in range(DEC_SEQ):
            src_ref[tok(t), :] = vs[t].astype(BF16)
        v8_ref[...] = _dot(gather_ref[:, 0:S_ROWS], src_ref[0:S_ROWS, :])

    slot = lax.broadcasted_iota(jnp.int32, (SUBLANES, HEAD), 0)
    ones8 = jnp.where((slot >= DEC_SEQ) & (slot < N_SRC), 1.0, 0.0).astype(BF16)
    base = pl.multiple_of(step * (SEQ_BLOCK * SUBLANES), SEQ_BLOCK * SUBLANES)
    for j in range(SEQ_BLOCK):
        rows = pl.ds(base + j * SUBLANES, SUBLANES)
        for h in range(N_HEADS):
            hl = slice(h * HEAD, (h + 1) * HEAD)
            st = s_ref[j, h]
            kd = kd8_ref[rows, hl].astype(BF16)
            o8_ref[rows, hl] = _dot(qd8_ref[rows, hl].astype(BF16), st.astype(BF16))
            so_ref[j, h] = _dot_tn(kd, ones8) * st + _dot_tn(kd, v8_ref[rows, hl].astype(BF16))

    @pl.when(step == pl.num_programs(0) - 1)
    def _():
        o8 = o8_ref[...]
        hi = o8.astype(BF16)
        lo = (o8 - hi.astype(F32)).astype(BF16)
        o = oin_ref[...] + _dot(scatter_ref[...], hi) + _dot(scatter_ref[...], lo)
        for h in range(N_HEADS):
            hl = slice(h * HEAD, (h + 1) * HEAD)
            oh = o[:, hl]
            oh = oh * lax.rsqrt(jnp.mean(oh * oh, axis=-1, keepdims=True) + EPS) * gnorm_ref[...]
            mix_ref[:, D_POOL + h * HEAD:D_POOL + (h + 1) * HEAD] = oh * _silu(z_ref[:, _rec_col(h, 3)])
        xo_ref[...] = x_ref[...] + _dot(mix_ref[...].astype(BF16), wout_ref[...])


def _sample_mixer(x, p, layer, state_pool, state_rec, acc, *, name):
    alias_specs, alias_args, n_alias = _stacked(acc)
    n_in = 10
    kern = functools.partial(_sample_mixer_kernel, n_alias=n_alias, layer=layer)
    rec_block = (None, SEQ_BLOCK, N_HEADS, HEAD, HEAD)
    return pl.pallas_call(
        kern,
        grid=(DEC_BATCH // SEQ_BLOCK,),
        in_specs=[
            _const_spec((S_ROWS, D_MODEL)),
            _layer_spec((1, D_MODEL), layer),
            _const_spec((D_MODEL, D_IN)),
            _layer_spec((len(POOL_WINDOWS), POOL_GROUP, POOL_GROUP), layer),
            _layer_spec((1, D_POOL), layer),
            _const_spec((DEPTH, D_REC)),
            _layer_spec((1, HEAD), layer),
            _const_spec((D_MODEL, D_MODEL)),
            _layer_spec((POOL_BUF, DEC_BATCH, D_POOL), layer),
            pl.BlockSpec(rec_block, lambda s: (layer, s, 0, 0, 0)),
        ] + alias_specs,
        out_specs=[
            pl.BlockSpec((S_ROWS, D_MODEL), lambda s: (0, 0)),
            pl.BlockSpec((None, POOL_BUF, DEC_BATCH, D_POOL), lambda s: (layer, 0, 0, 0)),
            pl.BlockSpec(rec_block, lambda s: (layer, s, 0, 0, 0)),
        ],
        out_shape=[
            jax.ShapeDtypeStruct((S_ROWS, D_MODEL), F32),
            jax.ShapeDtypeStruct((DEPTH, POOL_BUF, DEC_BATCH, D_POOL), F32),
            jax.ShapeDtypeStruct((DEPTH, DEC_BATCH, N_HEADS, HEAD, HEAD), F32),
        ],
        scratch_shapes=[
            pltpu.VMEM((S_ROWS, D_IN), F32),
            pltpu.VMEM((S_ROWS, D_MODEL), F32),
            pltpu.VMEM((S_ROWS, D_REC), F32),
            pltpu.VMEM((N_SRC * DEC_BATCH, D_REC), BF16),
            pltpu.VMEM((S_SLOT, N_SRC * DEC_BATCH), BF16),
            pltpu.VMEM((S_ROWS, S_SLOT), BF16),
            pltpu.VMEM((S_SLOT, D_REC), F32),
            pltpu.VMEM((S_SLOT, D_REC), F32),
            pltpu.VMEM((S_SLOT, D_REC), F32),
            pltpu.VMEM((S_SLOT, D_REC), F32),
        ],
        input_output_aliases={n_in + k: 1 + k for k in range(n_alias)},
        compiler_params=pltpu.CompilerParams(
            dimension_semantics=("arbitrary",), vmem_limit_bytes=BIG_VMEM_LIMIT),
        name=name,
    )(x, p["n1"], p["w_in"][layer], p["w_pool"], p["pool_scale"], p["lb_raw"], p["g_norm"], p["w_out"][layer],
      state_pool, state_rec, *alias_args)


FFN_BLOCK = MXU_COLS


def _sample_ffn_kernel(*refs, n_alias, final):
    x_ref, n2_ref, wa_ref, wb_ref, cw_ref, cb_ref, wd_ref, cs_ref, fg_ref = refs[:9]
    xo_ref, cso_ref, wao_ref, wbo_ref, wdo_ref, hn_ref, acc_ref = refs[9 + n_alias:]
    step = pl.program_id(0)
    tok = lambda t: slice(t * DEC_BATCH, (t + 1) * DEC_BATCH)

    @pl.when(step == 0)
    def _():
        hn_ref[...] = _rmsnorm(x_ref[...], n2_ref[...]).astype(BF16)
        acc_ref[...] = jnp.zeros(acc_ref.shape, F32)

    wa, wb, wd = wa_ref[...].astype(BF16), wb_ref[...].astype(BF16), wd_ref[...].astype(BF16)
    wao_ref[...] = wa
    wbo_ref[...] = wb
    wdo_ref[...] = wd
    hn = hn_ref[...]
    a = _dot(hn, wa)
    gate = _dot(hn, wb)
    ext = [cs_ref[:, j, :] for j in range(CONV_W - 1)] + [a[tok(t)] for t in range(DEC_SEQ)]
    acs = []
    for t in range(DEC_SEQ):
        ac = cb_ref[...]
        for j in range(CONV_W):
            ac = ac + cw_ref[j:j + 1, :] * ext[t + j]
        acs.append(ac)
    act = (_gelu(jnp.concatenate(acs, axis=0)) * gate).astype(BF16)
    acc_ref[...] += _dot(act, wd)
    for j in range(CONV_W - 1):
        cso_ref[:, j, :] = ext[DEC_SEQ + j]

    @pl.when(step == pl.num_programs(0) - 1)
    def _():
        xo = x_ref[...] + acc_ref[...]
        if final:
            xo = _rmsnorm(xo, fg_ref[...])
        xo_ref[...] = xo


def _sample_ffn(x, p, layer, state_conv, acc, *, final, name):
    alias_specs, alias_args, n_alias = _stacked(acc)
    n_in = 9
    kern = functools.partial(_sample_ffn_kernel, n_alias=n_alias, final=final)
    cols = lambda shape: pl.BlockSpec((None,) + shape, lambda j: (layer,) + (0,) * (len(shape) - 1) + (j,))
    return pl.pallas_call(
        kern,
        grid=(D_FF // FFN_BLOCK,),
        in_specs=[
            _const_spec((S_ROWS, D_MODEL)),
            _layer_spec((1, D_MODEL), layer),
            cols((D_MODEL, FFN_BLOCK)),
            cols((D_MODEL, FFN_BLOCK)),
            cols((CONV_W, FFN_BLOCK)),
            cols((1, FFN_BLOCK)),
            pl.BlockSpec((None, FFN_BLOCK, D_MODEL), lambda j: (layer, j, 0)),
            cols((DEC_BATCH, CONV_W - 1, FFN_BLOCK)),
            _const_spec((1, D_MODEL)),
        ] + alias_specs,
        out_specs=[
            pl.BlockSpec((S_ROWS, D_MODEL), lambda j: (0, 0)),
            cols((DEC_BATCH, CONV_W - 1, FFN_BLOCK)),
            pl.BlockSpec((D_MODEL, FFN_BLOCK), lambda j: (0, j)),
            pl.BlockSpec((D_MODEL, FFN_BLOCK), lambda j: (0, j)),
            pl.BlockSpec((FFN_BLOCK, D_MODEL), lambda j: (j, 0)),
        ],
        out_shape=[
            jax.ShapeDtypeStruct((S_ROWS, D_MODEL), F32),
            jax.ShapeDtypeStruct((DEPTH, DEC_BATCH, CONV_W - 1, D_FF), F32),
            jax.ShapeDtypeStruct((D_MODEL, D_FF), BF16),
            jax.ShapeDtypeStruct((D_MODEL, D_FF), BF16),
            jax.ShapeDtypeStruct((D_FF, D_MODEL), BF16),
        ],
        scratch_shapes=[
            pltpu.VMEM((S_ROWS, D_MODEL), BF16),
            pltpu.VMEM((S_ROWS, D_MODEL), F32),
        ],
        input_output_aliases={n_in + k: 1 + k for k in range(n_alias)},
        compiler_params=pltpu.CompilerParams(
            dimension_semantics=("arbitrary",), vmem_limit_bytes=VMEM_LIMIT),
        name=name,
    )(x, p["n2"], p["w_a"], p["w_b"], p["conv_w"], p["conv_b"], p["w_down"], state_conv, p["final_g"],
      *alias_args)


MIXER_TILE = 1024
MIXER_SUB = 256
MIXER_CHUNK = 64
FFN_TILE = 512


def kernel(x_prompt, x_sample, state_pool, state_rec, state_conv, meta_tokens, norm1_g, w_in, w_pool, pool_scale, lb_raw, rec_norm_g, w_out, norm2_g, w_a, w_b, conv_w, conv_b, w_down, final_g):
    p = dict(
        n1=norm1_g.reshape(DEPTH, 1, D_MODEL), w_in=[w_in[0].astype(BF16)],
        w_pool=w_pool.astype(BF16), pool_scale=pool_scale.reshape(DEPTH, 1, D_POOL), lb_raw=lb_raw,
        g_norm=rec_norm_g.reshape(DEPTH, 1, HEAD), w_out=[w_out[0].astype(BF16)],
        n2=norm2_g.reshape(DEPTH, 1, D_MODEL), w_a=w_a, w_b=w_b,
        conv_w=conv_w, conv_b=conv_b.reshape(DEPTH, 1, D_FF), w_down=w_down,
        final_g=final_g.reshape(1, D_MODEL),
    )
    last = DEPTH - 1
    n_seq = x_prompt.shape[0]
    zeros = lambda *shape: jnp.zeros((DEPTH,) + shape, F32)

    xp, xm = x_prompt, meta_tokens.reshape(1, N_META, D_MODEL)
    xs = x_sample.transpose(1, 0, 2).reshape(S_ROWS, D_MODEL)
    pool_rows = state_pool.transpose(0, 2, 1, 3)
    p_mix = [zeros(n_seq, POOL_BUF, D_POOL), zeros(n_seq, N_HEADS, HEAD, HEAD)]
    p_ffn = [zeros(n_seq, CONV_W - 1, D_FF)]
    sample_results = [((DEPTH, POOL_BUF, DEC_BATCH, D_POOL), 2), ((DEPTH, DEC_BATCH, N_HEADS, HEAD, HEAD), 1),
                      ((DEPTH, DEC_BATCH, CONV_W - 1, D_FF), 1)]
    for l in range(DEPTH):
        xp, xm, *p_mix = _prompt_mixer(xp, xm, p, l, p_mix, tile=MIXER_TILE, sub=MIXER_SUB, chunk=MIXER_CHUNK,
                                       name=f"prompt_mixer_{l}", zero_shapes=sample_results if l == 0 else ())
        if l == 0:
            p_mix, (pool_s, rec_s, conv_s) = p_mix[:2], p_mix[2:]
            s_mix, s_ffn = [pool_s, rec_s], [conv_s]
        xs, *s_mix = _sample_mixer(xs, p, l, pool_rows, state_rec, s_mix, name=f"sample_mixer_{l}")
        xs, conv_s, *ffn_weights = _sample_ffn(xs, p, l, state_conv, s_ffn, final=(l == last),
                                               name=f"sample_ffn_{l}")
        s_ffn = [conv_s]
        xp, xm, conv_p, *mixer_weights = _prompt_ffn(xp, xm, p, l, ffn_weights, p_ffn, tile=FFN_TILE,
                                                     final=(l == last), name=f"prompt_ffn_{l}",
                                                     to_round=(w_in, w_out) if l < last else ())
        p_ffn = [conv_p]
        for key, w in zip(("w_in", "w_out"), mixer_weights):
            p[key].append(w)
    pool_p, rec_p = p_mix
    conv_p, = p_ffn
    pool_s, rec_s = s_mix
    y_sample = xs.reshape(DEC_SEQ, DEC_BATCH, D_MODEL).transpose(1, 0, 2)

    return xp, y_sample, pool_p, rec_p, conv_p, pool_s.transpose(0, 2, 1, 3), rec_s, conv_s
```

```python
import functools

import jax
import jax.numpy as jnp
from jax import lax
from jax.experimental import pallas as pl
from jax.experimental.pallas import tpu as pltpu

F32 = jnp.float32
BF16 = jnp.bfloat16

D_MODEL = 1024
DEPTH = 4
D_POOL = 512
D_REC = 512
N_HEADS = 4
HEAD = 128
N_PARTS = 4
D_IN = D_POOL + N_PARTS * D_REC
D_FF = 2816
POOL_WINDOWS = (2, 4, 8, 16)
POOL_GROUP = 128
POOL_BUF = 15
N_META = 16
DEC_BATCH = 128
DEC_SEQ = 4
CONV_W = 3
EPS = 1e-6
Q_SCALE = HEAD ** -0.5
SQRT_HALF = 0.7071067811865476

SUBLANES = 8
BF16_ROWS = 16
MXU_COLS = 256
HEAD_GROUP = 2
LOCKSTEP_CHUNKS = 2
INTRA_ROUNDS_PER_FILLER = 20
HIST = 16
CONV_HIST = 8
VMEM_LIMIT = 56 * 1024 * 1024
BIG_VMEM_LIMIT = 62 * 1024 * 1024


def _rmsnorm(x, g):
    return x * lax.rsqrt(jnp.mean(x * x, axis=-1, keepdims=True) + EPS) * g


def _silu(x):
    return x * (1.0 / (1.0 + jnp.exp(-x)))


def _gelu(x):
    return 0.5 * x * (1.0 + lax.erf(x * SQRT_HALF))


def _lower_bound(lb_raw, layer):
    m = jnp.max(lb_raw, axis=0, keepdims=True)
    e = jnp.exp(lb_raw - m)
    p = e / jnp.sum(e, axis=0, keepdims=True)
    lb = jnp.zeros_like(m)
    for i in range(1, layer + 1):
        lb = lb + p[i:i + 1]
    return lb


def _forget_gate(f, lb):
    m = jnp.exp(-jnp.abs(f))
    r = 1.0 / (1.0 + m)
    sig_neg = jnp.where(f >= 0, m * r, r)
    log_sig = jnp.minimum(f, 0.0) - jnp.log(1.0 + m)
    a = jnp.log(lb)
    b = jnp.log(1.0 - lb) + log_sig
    log_f = jnp.maximum(a, b) + jnp.log(1.0 + jnp.exp(-jnp.abs(a - b)))
    return log_f, (1.0 - lb) * sig_neg


def _dot(a, b):
    return jnp.dot(a, b, preferred_element_type=F32)


def _dot_nt(a, b):
    return lax.dot_general(a, b, (((1,), (1,)), ((), ())), preferred_element_type=F32)


def _dot_tn(a, b):
    return lax.dot_general(a, b, (((0,), (0,)), ((), ())), preferred_element_type=F32)


HALF = SUBLANES // 2


def _gla_masks(n_blk):
    c_len = SUBLANES * n_blk
    lane = lax.broadcasted_iota(jnp.int32, (SUBLANES, c_len), 1)
    row = lax.broadcasted_iota(jnp.int32, (SUBLANES, c_len), 0)
    diag = lane - row
    near = [[diag == SUBLANES * i - r for r in range(HALF)] for i in range(n_blk)]
    band = [(diag <= SUBLANES * i) & (diag > SUBLANES * i - HALF) for i in range(n_blk)]
    across = [(row >= HALF) & (lane >= SUBLANES * i) & (lane < SUBLANES * i + HALF) for i in range(n_blk)]
    below = {c: lane < SUBLANES * c for c in range(1, n_blk)}
    return near, band, across, below


def _gla_intra(q_raw, f_raw, lb, v, n_blk, masks):
    near, band, across, below = masks
    c_len = SUBLANES * n_blk
    width = q_raw.shape[1]
    heads = [slice(lo, lo + HEAD) for lo in range(0, width, HEAD)]
    sub = lax.broadcasted_iota(jnp.int32, (SUBLANES, width), 0)
    blk = lambda a, i: a[SUBLANES * i:SUBLANES * (i + 1)]
    square = lambda x: _dot_nt(x, x)
    rows_bf16 = lambda tiles: jnp.concatenate(tiles, axis=0).astype(BF16)

    tot, q_in, k_out, halves, near_rows = [], [], [], [], []
    for i in range(n_blk):
        qi = _silu(blk(q_raw, i)) * Q_SCALE
        gi, ki = _forget_gate(blk(f_raw, i), lb)
        yield
        local = gi
        for s in (1, 2, 4):
            local = local + jnp.where(sub >= s, pltpu.roll(local, s, 0), 0.0)
        yield
        last = local[SUBLANES - 1:SUBLANES]
        tot.append(jnp.exp(last))
        q_in.append(qi * jnp.exp(local))
        k_out.append(ki * jnp.exp(last - local))
        mid = jnp.broadcast_to(local[HALF - 1:HALF], (SUBLANES, width))
        halves.append(jnp.where(sub >= HALF, qi, ki) * jnp.exp(-jnp.abs(local - mid)))
        yield
        gate = jnp.exp(gi)
        w, rows = qi, [jnp.zeros((SUBLANES, c_len), F32) for _ in heads]
        for r in range(HALF):
            if r > 0:
                w = w * (gate if r == 1 else pltpu.roll(gate, r - 1, 0))
            pair = w * (ki if r == 0 else pltpu.roll(ki, r, 0))
            for n, hl in enumerate(heads):
                rows[n] = jnp.where(near[i][r], jnp.sum(pair[:, hl], axis=-1, keepdims=True), rows[n])
            yield
        near_rows.append(rows)

    spans = {}

    def span(a, b):
        if a >= b:
            return None
        if (a, b) not in spans:
            if (a + 1, b) in spans:
                spans[a, b] = tot[a] * spans[a + 1, b]
            else:
                head = span(a, b - 1)
                spans[a, b] = tot[b - 1] if head is None else head * tot[b - 1]
        return spans[a, b]

    scaled = lambda x, s: x if s is None else x * s
    q_dec = rows_bf16([scaled(q_in[i], span(0, i)) for i in range(n_blk)])
    k_dec = rows_bf16([scaled(k_out[i], span(i + 1, n_blk)) for i in reversed(range(n_blk))][::-1])
    yield

    level_prod, level_row = {}, {}
    half = n_blk // 2
    while half >= 1:
        tiles, upper = {}, []
        for cut in range(half, n_blk, 2 * half):
            for i in range(cut, cut + half):
                tiles[i] = scaled(q_in[i], span(cut, i))
                upper.append(i)
            for i in reversed(range(cut - half, cut)):
                tiles[i] = scaled(k_out[i], span(i + 1, cut))
        both = rows_bf16([tiles[i] for i in range(n_blk)])
        if (len(upper) * SUBLANES) % BF16_ROWS:
            upper = list(range(n_blk))
        left = both if len(upper) == n_blk else rows_bf16([tiles[i] for i in upper])
        level_prod[half] = [_dot_nt(left[:, hl], both[:, hl]) for hl in heads]
        level_row[half] = {i: n for n, i in enumerate(upper)}
        half //= 2
        yield
    halves = rows_bf16(halves)
    half_prod = [square(halves[:, hl]) for hl in heads]
    yield

    v_b, decay, out = v.astype(BF16), span(0, n_blk), []
    for n, hl in enumerate(heads):
        rows = []
        for i in range(n_blk):
            row = jnp.where(band[i], near_rows[i][n], jnp.where(across[i], blk(half_prod[n], i), 0.0))
            half = 1
            while half < n_blk:
                if i % (2 * half) >= half:
                    prod = blk(level_prod[half][n], level_row[half][i])
                    row = jnp.where(below[(i // (2 * half)) * 2 * half + half], prod, row)
                half *= 2
            rows.append(row)
        out.append((_dot(rows_bf16(rows), v_b[:, hl]), q_dec[:, hl], k_dec[:, hl], decay[:, hl]))
        yield
    return out


def _gla_state(intra, v, st):
    o_intra, q_dec, k_dec, decay = intra
    out = o_intra + _dot_nt(q_dec, st.astype(BF16))
    return out, st * decay + _dot_tn(v.astype(BF16), k_dec)


def _pool_count(start, offset, rows, w):
    if start >= max(POOL_WINDOWS) - 1:
        return 1.0 / w
    pos = start + offset + lax.broadcasted_iota(jnp.int32, (rows, 1), 0)
    return 1.0 / jnp.minimum(pos + 1, w).astype(F32)


def _rec_col(h, part):
    lo = D_POOL + part * D_REC + h * HEAD
    return slice(lo, lo + HEAD)


def _layer_spec(shape, layer):
    zeros = (0,) * len(shape)
    return pl.BlockSpec((None,) + tuple(shape), lambda *_: (layer,) + zeros, pipeline_mode=pl.Buffered(1))


def _const_spec(shape):
    zeros = (0,) * len(shape)
    return pl.BlockSpec(tuple(shape), lambda *_: zeros, pipeline_mode=pl.Buffered(1))


def _stacked(acc):
    return [pl.BlockSpec(memory_space=pl.ANY)] * len(acc), list(acc), len(acc)


def _prompt_mixer_kernel(*refs, n_alias, n_zero, layer, tile, sub, chunk):
    (x_ref, xn_ref, xm_ref, n1_ref, win_ref, wpool_ref, pscale_ref, lbraw_ref, gnorm_ref, wout_ref) = refs[:10]
    xo_ref, xmo_ref, poolo_ref, so_ref = refs[10 + n_alias:14 + n_alias]
    zero_refs = refs[14 + n_alias:14 + n_alias + n_zero]
    (z0_ref, z1_ref, mix0_ref, mix1_ref, u_ref, st_ref, hn_ref, pool0_ref, s0_ref) = refs[14 + n_alias + n_zero:]
    z_refs, mix_refs = (z0_ref, z1_ref), (mix0_ref, mix1_ref)
    b, t, n_tiles = pl.program_id(0), pl.program_id(1), pl.num_programs(1)
    n_sub = tile // sub
    lb = _lower_bound(lbraw_ref[...], layer)
    gnorm = gnorm_ref[...]

    def project(x, slot, rows, start, offset):
        z_ref, mix_ref = z_refs[slot], mix_refs[slot]

        def norm():
            hn_ref[0:rows, :] = _rmsnorm(x(), n1_ref[...]).astype(BF16)

        def columns(n):
            cols = slice(n * MXU_COLS, (n + 1) * MXU_COLS)

            def step():
                z_ref[0:rows, cols] = _dot(hn_ref[0:rows, :], win_ref[:, cols])
            return step

        def pool():
            u_ref[HIST:HIST + rows, :] = z_ref[0:rows, 0:D_POOL]
            for gi, w in enumerate(POOL_WINDOWS):
                lanes = slice(gi * POOL_GROUP, (gi + 1) * POOL_GROUP)
                acc = u_ref[HIST:HIST + rows, lanes]
                u = acc
                for j in range(1, w):
                    acc = acc + u_ref[HIST - j:HIST - j + rows, lanes]
                m = acc * _pool_count(start, offset, rows, w) - u
                y = _dot(m.astype(BF16), wpool_ref[gi]) * pscale_ref[:, lanes]
                mix_ref[0:rows, lanes] = y.astype(BF16)
            u_ref[0:HIST, :] = u_ref[rows:rows + HIST, :]

        n_pool = D_POOL // MXU_COLS
        tiles = [columns(n) for n in range(D_IN // MXU_COLS)]
        return [norm] + tiles[:n_pool] + [pool] + tiles[n_pool:]

    def lockstep(routines, tick, done=lambda: False):
        live = list(routines)
        while live and not done():
            for routine in list(live):
                try:
                    next(routine)
                except StopIteration:
                    live.remove(routine)
            tick()
        return live

    def attend(x, slot, rows, chunk, state_ref, store, filler=()):
        z_ref, mix_ref = z_refs[slot], mix_refs[slot]
        n_blk, n_chunks = chunk // SUBLANES, rows // chunk
        masks = _gla_masks(n_blk)
        intra, ready = {}, set()
        filler = list(filler)
        rounds = [0]

        def tick(every):
            rounds[0] += 1
            if filler and rounds[0] % every == 0:
                filler.pop(0)()

        def within(c, g):
            span = slice(c * chunk, (c + 1) * chunk)
            lanes = slice(g * HEAD_GROUP * HEAD, (g + 1) * HEAD_GROUP * HEAD)
            cols = lambda n: slice(D_POOL + n * D_REC + lanes.start, D_POOL + n * D_REC + lanes.stop)
            per_head = yield from _gla_intra(z_ref[span, cols(0)], z_ref[span, cols(1)], lb[:, lanes],
                                             z_ref[span, cols(2)], n_blk, masks)
            for n in range(HEAD_GROUP):
                intra[g * HEAD_GROUP + n, c] = per_head[n]
            ready.add((c, g))

        def carry(h):
            for c in range(n_chunks):
                while (h, c) not in intra:
                    yield
                if c == 0:
                    st = state_ref[h]
                span = slice(c * chunk, (c + 1) * chunk)
                o, st = _gla_state(intra.pop((h, c)), z_ref[span, _rec_col(h, 2)], st)
                yield
                o = o * lax.rsqrt(jnp.mean(o * o, axis=-1, keepdims=True) + EPS) * gnorm
                o = o * _silu(z_ref[span, _rec_col(h, 3)])
                mix_ref[span, D_POOL + h * HEAD:D_POOL + (h + 1) * HEAD] = o.astype(BF16)
                yield
            state_ref[h] = st

        acc = _dot(mix_ref[0:rows, 0:D_POOL], wout_ref[0:D_POOL, :])
        running = [carry(h) for h in range(N_HEADS)]
        for lo in range(0, n_chunks, LOCKSTEP_CHUNKS):
            group = [(c, g) for c in range(lo, min(lo + LOCKSTEP_CHUNKS, n_chunks))
                     for g in range(N_HEADS // HEAD_GROUP)]
            running = lockstep([within(c, g) for c, g in group] + running, lambda: tick(INTRA_ROUNDS_PER_FILLER),
                               done=lambda: ready.issuperset(group))
        lockstep(running, lambda: tick(1))
        for lo in range(D_POOL, D_MODEL, MXU_COLS):
            acc = acc + _dot(mix_ref[0:rows, lo:lo + MXU_COLS], wout_ref[lo:lo + MXU_COLS, :])
            if filler:
                filler.pop(0)()
        while filler:
            filler.pop(0)()
        store(x() + acc)

    def emit(steps):
        for step in steps:
            step()

    def store_meta(y):
        xmo_ref[0] = y

    @pl.when((b == 0) & (t == 0))
    def _():
        u_ref[0:HIST, :] = jnp.zeros((HIST, D_POOL), F32)
        s0_ref[...] = jnp.zeros(s0_ref.shape, F32)
        xm = lambda: xm_ref[0]
        emit(project(xm, 0, N_META, 0, 0))
        pool0_ref[...] = u_ref[0:HIST, :]
        attend(xm, 0, N_META, N_META, s0_ref, store_meta)
        emit(project(lambda: x_ref[0, 0:sub, :], 0, sub, N_META, 0))

    @pl.when(t == 0)
    def _():
        st_ref[...] = s0_ref[...]

    for i in range(n_sub):
        r0 = i * sub
        if i == n_sub - 1:
            poolo_ref[0] = u_ref[HIST - POOL_BUF:HIST, :]
            u_ref[0:HIST, :] = jnp.where(t == n_tiles - 1, pool0_ref[...], u_ref[0:HIST, :])

            def x_next():
                return xn_ref[0]
        else:
            def x_next(r0=r0):
                return x_ref[0, r0 + sub:r0 + 2 * sub, :]

        def x(r0=r0):
            return x_ref[0, r0:r0 + sub, :]

        def store(y, r0=r0):
            xo_ref[0, r0:r0 + sub, :] = y

        attend(x, i % 2, sub, chunk, st_ref, store, filler=project(x_next, (i + 1) % 2, sub, N_META, 0))

    for zero_ref in zero_refs:
        zero_ref[...] = jnp.zeros(zero_ref.shape, zero_ref.dtype)

    @pl.when(t == n_tiles - 1)
    def _():
        for h in range(N_HEADS):
            so_ref[0, h] = st_ref[h].T


def _prompt_mixer(x, xm, p, layer, acc, *, tile, sub, chunk, name, zero_shapes=()):
    n_seq, t_len, _ = x.shape
    n_tiles, n_sub = t_len // tile, tile // sub
    assert t_len % tile == 0 and n_sub % 2 == 0 and sub % chunk == 0
    alias_specs, alias_args, n_alias = _stacked(acc)
    n_in = 10

    def next_rows(b, t):
        nxt = jnp.minimum(b * n_tiles + t + 1, n_seq * n_tiles - 1)
        return nxt // n_tiles, (nxt % n_tiles) * n_sub, 0

    per_layer = n_seq * n_tiles // DEPTH
    assert n_seq * n_tiles == per_layer * DEPTH
    zero_specs = []
    for shape, axis in zero_shapes:
        assert shape[0] == DEPTH and shape[axis] % per_layer == 0
        block = list(shape)
        block[0], block[axis] = 1, shape[axis] // per_layer

        def index(b, t, axis=axis, rank=len(shape)):
            step = b * n_tiles + t
            idx = [0] * rank
            idx[0], idx[axis] = step // per_layer, step % per_layer
            return tuple(idx)

        zero_specs.append(pl.BlockSpec(tuple(block), index))
    kern = functools.partial(_prompt_mixer_kernel, n_alias=n_alias, n_zero=len(zero_shapes), layer=layer,
                             tile=tile, sub=sub, chunk=chunk)
    return pl.pallas_call(
        kern,
        grid=(n_seq, n_tiles),
        in_specs=[
            pl.BlockSpec((1, tile, D_MODEL), lambda b, t: (b, t, 0)),
            pl.BlockSpec((1, sub, D_MODEL), next_rows),
            _const_spec((1, N_META, D_MODEL)),
            _layer_spec((1, D_MODEL), layer),
            _const_spec((D_MODEL, D_IN)),
            _layer_spec((len(POOL_WINDOWS), POOL_GROUP, POOL_GROUP), layer),
            _layer_spec((1, D_POOL), layer),
            _const_spec((DEPTH, D_REC)),
            _layer_spec((1, HEAD), layer),
            _const_spec((D_MODEL, D_MODEL)),
        ] + alias_specs,
        out_specs=[
            pl.BlockSpec((1, tile, D_MODEL), lambda b, t: (b, t, 0)),
            pl.BlockSpec((1, N_META, D_MODEL), lambda b, t: (0, 0, 0)),
            pl.BlockSpec((None, 1, POOL_BUF, D_POOL), lambda b, t: (layer, b, 0, 0)),
            pl.BlockSpec((None, 1, N_HEADS, HEAD, HEAD), lambda b, t: (layer, b, 0, 0, 0)),
        ] + zero_specs,
        out_shape=[
            jax.ShapeDtypeStruct((n_seq, t_len, D_MODEL), F32),
            jax.ShapeDtypeStruct((1, N_META, D_MODEL), F32),
            jax.ShapeDtypeStruct((DEPTH, n_seq, POOL_BUF, D_POOL), F32),
            jax.ShapeDtypeStruct((DEPTH, n_seq, N_HEADS, HEAD, HEAD), F32),
        ] + [jax.ShapeDtypeStruct(shape, F32) for shape, _ in zero_shapes],
        scratch_shapes=[
            pltpu.VMEM((sub, D_IN), F32),
            pltpu.VMEM((sub, D_IN), F32),
            pltpu.VMEM((sub, D_MODEL), BF16),
            pltpu.VMEM((sub, D_MODEL), BF16),
            pltpu.VMEM((HIST + sub, D_POOL), F32),
            pltpu.VMEM((N_HEADS, HEAD, HEAD), F32),
            pltpu.VMEM((sub, D_MODEL), BF16),
            pltpu.VMEM((HIST, D_POOL), F32),
            pltpu.VMEM((N_HEADS, HEAD, HEAD), F32),
        ],
        input_output_aliases={n_in + k: 2 + k for k in range(n_alias)},
        compiler_params=pltpu.CompilerParams(
            dimension_semantics=("arbitrary", "arbitrary"),
            vmem_limit_bytes=BIG_VMEM_LIMIT if zero_shapes else VMEM_LIMIT),
        name=name,
    )(x, x, xm, p["n1"], p["w_in"][layer], p["w_pool"], p["pool_scale"], p["lb_raw"], p["g_norm"], p["w_out"][layer],
      *alias_args)


def _prompt_ffn_kernel(*refs, n_alias, n_round, tile, final):
    x_ref, xm_ref, n2_ref, wa_ref, wb_ref, cw_ref, cb_ref, wd_ref, fg_ref = refs[:9]
    f32_refs = refs[9:9 + n_round]
    xo_ref, xmo_ref, convo_ref = refs[9 + n_round + n_alias:12 + n_round + n_alias]
    bf16_refs = refs[12 + n_round + n_alias:12 + 2 * n_round + n_alias]
    a_ref, conv0_ref = refs[12 + 2 * n_round + n_alias:]
    b, t = pl.program_id(0), pl.program_id(1)
    for f32_ref, bf16_ref in zip(f32_refs, bf16_refs):
        bf16_ref[...] = f32_ref[...].astype(BF16)
    lo = CONV_HIST - (CONV_W - 1)

    def ffn(x, rows):
        hn = _rmsnorm(x, n2_ref[...]).astype(BF16)
        a_ref[CONV_HIST:CONV_HIST + rows, :] = _dot(hn, wa_ref[...])
        gate = _dot(hn, wb_ref[...])
        ac = cb_ref[...]
        for j in range(CONV_W):
            ac = ac + cw_ref[j:j + 1, :] * a_ref[lo + j:lo + j + rows, :]
        act = (_gelu(ac) * gate).astype(BF16)
        return x + _dot(act, wd_ref[...])

    @pl.when((b == 0) & (t == 0))
    def _():
        a_ref[lo:CONV_HIST, :] = jnp.zeros((CONV_W - 1, D_FF), F32)
        xmo_ref[0] = ffn(xm_ref[0], N_META)
        conv0_ref[...] = a_ref[lo + N_META:CONV_HIST + N_META, :]

    @pl.when(t == 0)
    def _():
        a_ref[lo:CONV_HIST, :] = conv0_ref[...]

    xo = ffn(x_ref[0], tile)
    if final:
        xo = _rmsnorm(xo, fg_ref[...])
    xo_ref[0] = xo

    @pl.when(t == pl.num_programs(1) - 1)
    def _():
        convo_ref[0] = a_ref[lo + tile:CONV_HIST + tile, :]

    a_ref[lo:CONV_HIST, :] = a_ref[lo + tile:CONV_HIST + tile, :]


def _prompt_ffn(x, xm, p, layer, weights, acc, *, tile, final, name, to_round=()):
    n_seq, t_len, _ = x.shape
    n_tiles = t_len // tile
    w_a, w_b, w_down = weights
    alias_specs, alias_args, n_alias = _stacked(acc)
    n_round = len(to_round)
    n_in = 9 + n_round
    steps = n_seq * n_tiles
    for w in to_round:
        assert w.shape[1] % (steps * BF16_ROWS) == 0
    round_in = [pl.BlockSpec((None, w.shape[1] // steps, w.shape[2]), lambda b, t: (layer + 1, b * n_tiles + t, 0))
                for w in to_round]
    round_out = [pl.BlockSpec((w.shape[1] // steps, w.shape[2]), lambda b, t: (b * n_tiles + t, 0))
                 for w in to_round]
    kern = functools.partial(_prompt_ffn_kernel, n_alias=n_alias, n_round=n_round, tile=tile, final=final)
    return pl.pallas_call(
        kern,
        grid=(n_seq, t_len // tile),
        in_specs=[
            pl.BlockSpec((1, tile, D_MODEL), lambda b, t: (b, t, 0)),
            _const_spec((1, N_META, D_MODEL)),
            _layer_spec((1, D_MODEL), layer),
            _const_spec((D_MODEL, D_FF)),
            _const_spec((D_MODEL, D_FF)),
            _layer_spec((CONV_W, D_FF), layer),
            _layer_spec((1, D_FF), layer),
            _const_spec((D_FF, D_MODEL)),
            _const_spec((1, D_MODEL)),
        ] + round_in + alias_specs,
        out_specs=[
            pl.BlockSpec((1, tile, D_MODEL), lambda b, t: (b, t, 0)),
            pl.BlockSpec((1, N_META, D_MODEL), lambda b, t: (0, 0, 0)),
            pl.BlockSpec((None, 1, CONV_W - 1, D_FF), lambda b, t: (layer, b, 0, 0)),
        ] + round_out,
        out_shape=[
            jax.ShapeDtypeStruct((n_seq, t_len, D_MODEL), F32),
            jax.ShapeDtypeStruct((1, N_META, D_MODEL), F32),
            jax.ShapeDtypeStruct((DEPTH, n_seq, CONV_W - 1, D_FF), F32),
        ] + [jax.ShapeDtypeStruct(w.shape[1:], BF16) for w in to_round],
        scratch_shapes=[
            pltpu.VMEM((CONV_HIST + tile, D_FF), F32),
            pltpu.VMEM((CONV_W - 1, D_FF), F32),
        ],
        input_output_aliases={n_in + k: 2 + k for k in range(n_alias)},
        compiler_params=pltpu.CompilerParams(
            dimension_semantics=("arbitrary", "arbitrary"), vmem_limit_bytes=VMEM_LIMIT),
        name=name,
    )(x, xm, p["n2"], w_a, w_b, p["conv_w"], p["conv_b"], w_down, p["final_g"], *to_round, *alias_args)


S_ROWS = DEC_BATCH * DEC_SEQ
S_SLOT = SUBLANES * DEC_BATCH
N_SRC = DEC_SEQ + 3
SEQ_BLOCK = 16


def _sample_mixer_kernel(*refs, n_alias, layer):
    (x_ref, n1_ref, win_ref, wpool_ref, pscale_ref, lbraw_ref, gnorm_ref, wout_ref, sp_ref, s_ref) = refs[:10]
    (xo_ref, spo_ref, so_ref, z_ref, mix_ref, oin_ref, src_ref, gather_ref, scatter_ref,
     qd8_ref, kd8_ref, v8_ref, o8_ref) = refs[10 + n_alias:]
    step = pl.program_id(0)
    tok = lambda t: slice(t * DEC_BATCH, (t + 1) * DEC_BATCH)
    rec = lambda part: [_rec_col(h, part) for h in range(N_HEADS)]

    def heads(rows, part):
        return jnp.concatenate([z_ref[rows, c] for c in rec(part)], axis=1)

    @pl.when(step == 0)
    def _():
        hn = _rmsnorm(x_ref[...], n1_ref[...]).astype(BF16)
        z_ref[...] = _dot(hn, win_ref[...])

        def ext(j, lanes):
            if j >= POOL_BUF:
                return z_ref[tok(j - POOL_BUF), lanes]
            return sp_ref[j, :, lanes]

        for gi, w in enumerate(POOL_WINDOWS):
            lanes = slice(gi * POOL_GROUP, (gi + 1) * POOL_GROUP)
            ms = []
            for t in range(DEC_SEQ):
                acc = ext(POOL_BUF + t, lanes)
                u = acc
                for j in range(1, w):
                    acc = acc + ext(POOL_BUF + t - j, lanes)
                ms.append(acc * (1.0 / w) - u)
            y = _dot(jnp.concatenate(ms, axis=0).astype(BF16), wpool_ref[gi]) * pscale_ref[:, lanes]
            mix_ref[:, lanes] = y
        full = slice(0, D_POOL)
        for j in range(POOL_BUF):
            spo_ref[j] = ext(j + DEC_SEQ, full)

        lb = _lower_bound(lbraw_ref[...], layer)
        qs, ks, gs, vs = [], [], [], []
        for t in range(DEC_SEQ):
            qs.append(_silu(heads(tok(t), 0)) * Q_SCALE)
            log_f, k = _forget_gate(heads(tok(t), 1), lb)
            ks.append(k)
            gs.append(log_f if t == 0 else gs[-1] + log_f)
            vs.append(heads(tok(t), 2))
        g_last = gs[-1]
        for t in range(DEC_SEQ):
            acc = None
            for s in range(t + 1):
                prod = qs[t] * ks[s]
                if s < t:
                    prod = prod * jnp.exp(gs[t] - gs[s])
                parts = []
                for h in range(N_HEADS):
                    hl = slice(h * HEAD, (h + 1) * HEAD)
                    a = jnp.sum(prod[:, hl], axis=-1, keepdims=True)
                    parts.append(a * vs[s][:, hl])
                term = jnp.concatenate(parts, axis=1)
                acc = term if acc is None else acc + term
            oin_ref[tok(t), :] = acc

        row = lax.broadcasted_iota(jnp.int32, (S_SLOT, N_SRC * DEC_BATCH), 0)
        col = lax.broadcasted_iota(jnp.int32, (S_SLOT, N_SRC * DEC_BATCH), 1)
        hit = ((row & (SUBLANES - 1)) == (col >> 7)) & ((row >> 3) == (col & (DEC_BATCH - 1)))
        gather_ref[...] = jnp.where(hit, 1.0, 0.0).astype(BF16)
        row = lax.broadcasted_iota(jnp.int32, (S_ROWS, S_SLOT), 0)
        col = lax.broadcasted_iota(jnp.int32, (S_ROWS, S_SLOT), 1)
        hit = ((col & (SUBLANES - 1)) == (row >> 7)) & ((col >> 3) == (row & (DEC_BATCH - 1)))
        scatter_ref[...] = jnp.where(hit, 1.0, 0.0).astype(BF16)

        for t in range(DEC_SEQ):
            src_ref[tok(t), :] = (ks[t] * jnp.exp(g_last - gs[t])).astype(BF16)
        rest = jnp.exp(g_last)
        for part in range(3):
            piece = rest.astype(BF16)
            src_ref[tok(DEC_SEQ + part), :] = piece
            rest = rest - piece.astype(F32)
        kd8_ref[...] = _dot(gather_ref[...], src_ref[...])
        for t in range(DEC_SEQ):
            src_ref[tok(t), :] = (qs[t] * jnp.exp(gs[t])).astype(BF16)
        qd8_ref[...] = _dot(gather_ref[:, 0:S_ROWS], src_ref[0:S_ROWS, :])
        for t in range(DEC_SEQ):
            src_ref[tok(t), :] = vs[t].astype(BF16)
        v8_ref[...] = _dot(gather_ref[:, 0:S_ROWS], src_ref[0:S_ROWS, :])

    slot = lax.broadcasted_iota(jnp.int32, (SUBLANES, HEAD), 0)
    ones8 = jnp.where((slot >= DEC_SEQ) & (slot < N_SRC), 1.0, 0.0).astype(BF16)
    base = pl.multiple_of(step * (SEQ_BLOCK * SUBLANES), SEQ_BLOCK * SUBLANES)
    for j in range(SEQ_BLOCK):
        rows = pl.ds(base + j * SUBLANES, SUBLANES)
        for h in range(N_HEADS):
            hl = slice(h * HEAD, (h + 1) * HEAD)
            st = s_ref[j, h]
            kd = kd8_ref[rows, hl].astype(BF16)
            o8_ref[rows, hl] = _dot(qd8_ref[rows, hl].astype(BF16), st.astype(BF16))
            so_ref[j, h] = _dot_tn(kd, ones8) * st + _dot_tn(kd, v8_ref[rows, hl].astype(BF16))

    @pl.when(step == pl.num_programs(0) - 1)
    def _():
        o8 = o8_ref[...]
        hi = o8.astype(BF16)
        lo = (o8 - hi.astype(F32)).astype(BF16)
        o = oin_ref[...] + _dot(scatter_ref[...], hi) + _dot(scatter_ref[...], lo)
        for h in range(N_HEADS):
            hl = slice(h * HEAD, (h + 1) * HEAD)
            oh = o[:, hl]
            oh = oh * lax.rsqrt(jnp.mean(oh * oh, axis=-1, keepdims=True) + EPS) * gnorm_ref[...]
            mix_ref[:, D_POOL + h * HEAD:D_POOL + (h + 1) * HEAD] = oh * _silu(z_ref[:, _rec_col(h, 3)])
        xo_ref[...] = x_ref[...] + _dot(mix_ref[...].astype(BF16), wout_ref[...])


def _sample_mixer(x, p, layer, state_pool, state_rec, acc, *, name):
    alias_specs, alias_args, n_alias = _stacked(acc)
    n_in = 10
    kern = functools.partial(_sample_mixer_kernel, n_alias=n_alias, layer=layer)
    rec_block = (None, SEQ_BLOCK, N_HEADS, HEAD, HEAD)
    return pl.pallas_call(
        kern,
        grid=(DEC_BATCH // SEQ_BLOCK,),
        in_specs=[
            _const_spec((S_ROWS, D_MODEL)),
            _layer_spec((1, D_MODEL), layer),
            _const_spec((D_MODEL, D_IN)),
            _layer_spec((len(POOL_WINDOWS), POOL_GROUP, POOL_GROUP), layer),
            _layer_spec((1, D_POOL), layer),
            _const_spec((DEPTH, D_REC)),
            _layer_spec((1, HEAD), layer),
            _const_spec((D_MODEL, D_MODEL)),
            _layer_spec((POOL_BUF, DEC_BATCH, D_POOL), layer),
            pl.BlockSpec(rec_block, lambda s: (layer, s, 0, 0, 0)),
        ] + alias_specs,
        out_specs=[
            pl.BlockSpec((S_ROWS, D_MODEL), lambda s: (0, 0)),
            pl.BlockSpec((None, POOL_BUF, DEC_BATCH, D_POOL), lambda s: (layer, 0, 0, 0)),
            pl.BlockSpec(rec_block, lambda s: (layer, s, 0, 0, 0)),
        ],
        out_shape=[
            jax.ShapeDtypeStruct((S_ROWS, D_MODEL), F32),
            jax.ShapeDtypeStruct((DEPTH, POOL_BUF, DEC_BATCH, D_POOL), F32),
            jax.ShapeDtypeStruct((DEPTH, DEC_BATCH, N_HEADS, HEAD, HEAD), F32),
        ],
        scratch_shapes=[
            pltpu.VMEM((S_ROWS, D_IN), F32),
            pltpu.VMEM((S_ROWS, D_MODEL), F32),
            pltpu.VMEM((S_ROWS, D_REC), F32),
            pltpu.VMEM((N_SRC * DEC_BATCH, D_REC), BF16),
            pltpu.VMEM((S_SLOT, N_SRC * DEC_BATCH), BF16),
            pltpu.VMEM((S_ROWS, S_SLOT), BF16),
            pltpu.VMEM((S_SLOT, D_REC), F32),
            pltpu.VMEM((S_SLOT, D_REC), F32),
            pltpu.VMEM((S_SLOT, D_REC), F32),
            pltpu.VMEM((S_SLOT, D_REC), F32),
        ],
        input_output_aliases={n_in + k: 1 + k for k in range(n_alias)},
        compiler_params=pltpu.CompilerParams(
            dimension_semantics=("arbitrary",), vmem_limit_bytes=BIG_VMEM_LIMIT),
        name=name,
    )(x, p["n1"], p["w_in"][layer], p["w_pool"], p["pool_scale"], p["lb_raw"], p["g_norm"], p["w_out"][layer],
      state_pool, state_rec, *alias_args)


FFN_BLOCK = MXU_COLS


def _sample_ffn_kernel(*refs, n_alias, final):
    x_ref, n2_ref, wa_ref, wb_ref, cw_ref, cb_ref, wd_ref, cs_ref, fg_ref = refs[:9]
    xo_ref, cso_ref, wao_ref, wbo_ref, wdo_ref, hn_ref, acc_ref = refs[9 + n_alias:]
    step = pl.program_id(0)
    tok = lambda t: slice(t * DEC_BATCH, (t + 1) * DEC_BATCH)

    @pl.when(step == 0)
    def _():
        hn_ref[...] = _rmsnorm(x_ref[...], n2_ref[...]).astype(BF16)
        acc_ref[...] = jnp.zeros(acc_ref.shape, F32)

    wa, wb, wd = wa_ref[...].astype(BF16), wb_ref[...].astype(BF16), wd_ref[...].astype(BF16)
    wao_ref[...] = wa
    wbo_ref[...] = wb
    wdo_ref[...] = wd
    hn = hn_ref[...]
    a = _dot(hn, wa)
    gate = _dot(hn, wb)
    ext = [cs_ref[:, j, :] for j in range(CONV_W - 1)] + [a[tok(t)] for t in range(DEC_SEQ)]
    acs = []
    for t in range(DEC_SEQ):
        ac = cb_ref[...]
        for j in range(CONV_W):
            ac = ac + cw_ref[j:j + 1, :] * ext[t + j]
        acs.append(ac)
    act = (_gelu(jnp.concatenate(acs, axis=0)) * gate).astype(BF16)
    acc_ref[...] += _dot(act, wd)
    for j in range(CONV_W - 1):
        cso_ref[:, j, :] = ext[DEC_SEQ + j]

    @pl.when(step == pl.num_programs(0) - 1)
    def _():
        xo = x_ref[...] + acc_ref[...]
        if final:
            xo = _rmsnorm(xo, fg_ref[...])
        xo_ref[...] = xo


def _sample_ffn(x, p, layer, state_conv, acc, *, final, name):
    alias_specs, alias_args, n_alias = _stacked(acc)
    n_in = 9
    kern = functools.partial(_sample_ffn_kernel, n_alias=n_alias, final=final)
    cols = lambda shape: pl.BlockSpec((None,) + shape, lambda j: (layer,) + (0,) * (len(shape) - 1) + (j,))
    return pl.pallas_call(
        kern,
        grid=(D_FF // FFN_BLOCK,),
        in_specs=[
            _const_spec((S_ROWS, D_MODEL)),
            _layer_spec((1, D_MODEL), layer),
            cols((D_MODEL, FFN_BLOCK)),
            cols((D_MODEL, FFN_BLOCK)),
            cols((CONV_W, FFN_BLOCK)),
            cols((1, FFN_BLOCK)),
            pl.BlockSpec((None, FFN_BLOCK, D_MODEL), lambda j: (layer, j, 0)),
            cols((DEC_BATCH, CONV_W - 1, FFN_BLOCK)),
            _const_spec((1, D_MODEL)),
        ] + alias_specs,
        out_specs=[
            pl.BlockSpec((S_ROWS, D_MODEL), lambda j: (0, 0)),
            cols((DEC_BATCH, CONV_W - 1, FFN_BLOCK)),
            pl.BlockSpec((D_MODEL, FFN_BLOCK), lambda j: (0, j)),
            pl.BlockSpec((D_MODEL, FFN_BLOCK), lambda j: (0, j)),
            pl.BlockSpec((FFN_BLOCK, D_MODEL), lambda j: (j, 0)),
        ],
        out_shape=[
            jax.ShapeDtypeStruct((S_ROWS, D_MODEL), F32),
            jax.ShapeDtypeStruct((DEPTH, DEC_BATCH, CONV_W - 1, D_FF), F32),
            jax.ShapeDtypeStruct((D_MODEL, D_FF), BF16),
            jax.ShapeDtypeStruct((D_MODEL, D_FF), BF16),
            jax.ShapeDtypeStruct((D_FF, D_MODEL), BF16),
        ],
        scratch_shapes=[
            pltpu.VMEM((S_ROWS, D_MODEL), BF16),
            pltpu.VMEM((S_ROWS, D_MODEL), F32),
        ],
        input_output_aliases={n_in + k: 1 + k for k in range(n_alias)},
        compiler_params=pltpu.CompilerParams(
            dimension_semantics=("arbitrary",), vmem_limit_bytes=VMEM_LIMIT),
        name=name,
    )(x, p["n2"], p["w_a"], p["w_b"], p["conv_w"], p["conv_b"], p["w_down"], state_conv, p["final_g"],
      *alias_args)


MIXER_TILE = 1024
MIXER_SUB = 256
MIXER_CHUNK = 64
FFN_TILE = 512


def kernel(x_prompt, x_sample, state_pool, state_rec, state_conv, meta_tokens, norm1_g, w_in, w_pool, pool_scale, lb_raw, rec_norm_g, w_out, norm2_g, w_a, w_b, conv_w, conv_b, w_down, final_g):
    p = dict(
        n1=norm1_g.reshape(DEPTH, 1, D_MODEL), w_in=[w_in[0].astype(BF16)],
        w_pool=w_pool.astype(BF16), pool_scale=pool_scale.reshape(DEPTH, 1, D_POOL), lb_raw=lb_raw,
        g_norm=rec_norm_g.reshape(DEPTH, 1, HEAD), w_out=[w_out[0].astype(BF16)],
        n2=norm2_g.reshape(DEPTH, 1, D_MODEL), w_a=w_a, w_b=w_b,
        conv_w=conv_w, conv_b=conv_b.reshape(DEPTH, 1, D_FF), w_down=w_down,
        final_g=final_g.reshape(1, D_MODEL),
    )
    last = DEPTH - 1
    n_seq = x_prompt.shape[0]
    zeros = lambda *shape: jnp.zeros((DEPTH,) + shape, F32)

    xp, xm = x_prompt, meta_tokens.reshape(1, N_META, D_MODEL)
    xs = x_sample.transpose(1, 0, 2).reshape(S_ROWS, D_MODEL)
    pool_rows = state_pool.transpose(0, 2, 1, 3)
    p_mix = [zeros(n_seq, POOL_BUF, D_POOL), zeros(n_seq, N_HEADS, HEAD, HEAD)]
    p_ffn = [zeros(n_seq, CONV_W - 1, D_FF)]
    sample_results = [((DEPTH, POOL_BUF, DEC_BATCH, D_POOL), 2), ((DEPTH, DEC_BATCH, N_HEADS, HEAD, HEAD), 1),
                      ((DEPTH, DEC_BATCH, CONV_W - 1, D_FF), 1)]
    for l in range(DEPTH):
        xp, xm, *p_mix = _prompt_mixer(xp, xm, p, l, p_mix, tile=MIXER_TILE, sub=MIXER_SUB, chunk=MIXER_CHUNK,
                                       name=f"prompt_mixer_{l}", zero_shapes=sample_results if l == 0 else ())
        if l == 0:
            p_mix, (pool_s, rec_s, conv_s) = p_mix[:2], p_mix[2:]
            s_mix, s_ffn = [pool_s, rec_s], [conv_s]
        xs, *s_mix = _sample_mixer(xs, p, l, pool_rows, state_rec, s_mix, name=f"sample_mixer_{l}")
        xs, conv_s, *ffn_weights = _sample_ffn(xs, p, l, state_conv, s_ffn, final=(l == last),
                                               name=f"sample_ffn_{l}")
        s_ffn = [conv_s]
        xp, xm, conv_p, *mixer_weights = _prompt_ffn(xp, xm, p, l, ffn_weights, p_ffn, tile=FFN_TILE,
                                                     final=(l == last), name=f"prompt_ffn_{l}",
                                                     to_round=(w_in, w_out) if l < last else ())
        p_ffn = [conv_p]
        for key, w in zip(("w_in", "w_out"), mixer_weights):
            p[key].append(w)
    pool_p, rec_p = p_mix
    conv_p, = p_ffn
    pool_s, rec_s = s_mix
    y_sample = xs.reshape(DEC_SEQ, DEC_BATCH, D_MODEL).transpose(1, 0, 2)

    return xp, y_sample, pool_p, rec_p, conv_p, pool_s.transpose(0, 2, 1, 3), rec_s, conv_s
```
